```python
import functools
import jax, jax.numpy as jnp
from jax import lax
import numpy as np

D_MODEL = 1024
BATCH = 16
SEQ = 2048
DEPTH = 1
DEC_BATCH = 128
DEC_SEQ = 1
PAST_LEN = 16384
PAGE_SIZE = 128

N_HEADS = 16
N_KV_HEADS = 4
HEAD_DIM = 64
GROUP = N_HEADS // N_KV_HEADS
D_ATTN = N_HEADS * HEAD_DIM
D_KV = N_KV_HEADS * HEAD_DIM
WINDOW = 128
BLOCK = WINDOW
D_RNN = D_MODEL
N_RNN_BLOCKS = 16
RNN_BLOCK = D_RNN // N_RNN_BLOCKS
CONV_WIDTH = 4
LRU_C = 8.0
D_FF = 4 * D_MODEL
SPLITS = [D_ATTN, D_ATTN + D_KV, D_ATTN + 2 * D_KV, D_ATTN + 2 * D_KV + D_RNN,
          D_ATTN + 2 * D_KV + 2 * D_RNN, D_ATTN + 2 * D_KV + 2 * D_RNN + D_MODEL]
D_IN = D_ATTN + 2 * D_KV + 2 * D_RNN + 2 * D_MODEL
EPS = 1e-6

kernel_name = 'hybrid_swa_sink_rglru_decode_step'


def rmsnorm(x, g):
    xf = x.astype(jnp.float32)
    y = xf * lax.rsqrt(jnp.mean(xf * xf, axis=-1, keepdims=True) + EPS) * g.astype(jnp.float32)
    return y.astype(x.dtype)


def sink_softmax(s, mask, sink):
    s = jnp.where(mask, s, -jnp.inf)
    m = jnp.maximum(jnp.max(s, axis=-1, keepdims=True), sink)
    p = jnp.exp(s - m)
    return p / (jnp.sum(p, axis=-1, keepdims=True) + jnp.exp(sink - m))


def window_attention_prompt(q, k, v, sinks):
    B, T = q.shape[:2]
    nb = T // BLOCK
    qb = q.reshape(B, nb, BLOCK, N_KV_HEADS, GROUP, HEAD_DIM)
    kb = k.reshape(B, nb, BLOCK, N_KV_HEADS, HEAD_DIM)
    vb = v.reshape(B, nb, BLOCK, N_KV_HEADS, HEAD_DIM)
    pad = ((0, 0), (1, 0), (0, 0), (0, 0), (0, 0))
    kk = jnp.concatenate([jnp.pad(kb, pad)[:, :-1], kb], axis=2)
    vv = jnp.concatenate([jnp.pad(vb, pad)[:, :-1], vb], axis=2)
    s = jnp.einsum('bnqhgd,bnkhd->bnhgqk', qb, kk,
                   preferred_element_type=jnp.float32) * (HEAD_DIM ** -0.5)
    qi = jnp.arange(BLOCK)[:, None]
    kj = jnp.arange(2 * BLOCK)[None, :]
    diff = qi + BLOCK - kj
    band = (diff >= 0) & (diff <= WINDOW)
    has_prev = (jnp.arange(nb)[:, None] > 0) | (kj >= BLOCK)
    mask = (band[None] & has_prev[:, None, :])[None, :, None, None]
    sink = sinks.astype(jnp.float32).reshape(N_KV_HEADS, GROUP)[None, None, :, :, None, None]
    p = sink_softmax(s, mask, sink)
    o = jnp.einsum('bnhgqk,bnkhd->bnqhgd', p.astype(v.dtype), vv)
    return o.reshape(B, T, D_ATTN), k[:, -WINDOW:], v[:, -WINDOW:]


def window_attention_sample(k_buf, v_buf, q, k, v, sinks):
    B, T = q.shape[:2]
    qh = q.reshape(B, T, N_KV_HEADS, GROUP, HEAD_DIM)
    kk = jnp.concatenate([k_buf.astype(k.dtype), k], axis=1)
    vv = jnp.concatenate([v_buf.astype(v.dtype), v], axis=1)
    s = jnp.einsum('bqhgd,bkhd->bhgqk', qh, kk,
                   preferred_element_type=jnp.float32) * (HEAD_DIM ** -0.5)
    diff = jnp.arange(T)[:, None] + WINDOW - jnp.arange(WINDOW + T)[None, :]
    mask = ((diff >= 0) & (diff <= WINDOW))[None, None, None]
    sink = sinks.astype(jnp.float32).reshape(N_KV_HEADS, GROUP)[None, :, :, None, None]
    p = sink_softmax(s, mask, sink)
    o = jnp.einsum('bhgqk,bkhd->bqhgd', p.astype(v.dtype), vv)
    return o.reshape(B, T, D_ATTN), kk[:, -WINDOW:], vv[:, -WINDOW:]


def causal_conv(u, prev, conv_w, conv_b):
    T = u.shape[1]
    up = jnp.concatenate([prev.astype(u.dtype), u], axis=1)
    out = conv_b + sum(up[:, i:i + T] * conv_w[i] for i in range(CONV_WIDTH))
    return out, up[:, -(CONV_WIDTH - 1):]


def rg_lru(x, h0, positions, w_a, b_a, w_x, b_x, lam):
    B, T, _ = x.shape
    xf = x.astype(jnp.float32)
    xb = xf.reshape(B, T, N_RNN_BLOCKS, RNN_BLOCK)
    r = jax.nn.sigmoid(jnp.einsum('btnc,ncd->btnd', xb, w_a.astype(jnp.float32)).reshape(B, T, D_RNN)
                       + b_a.astype(jnp.float32))
    i = jax.nn.sigmoid(jnp.einsum('btnc,ncd->btnd', xb, w_x.astype(jnp.float32)).reshape(B, T, D_RNN)
                       + b_x.astype(jnp.float32))
    log_a = -LRU_C * r * jax.nn.softplus(-lam.astype(jnp.float32))
    a = jnp.exp(log_a)
    mult = jnp.where((positions == 0)[None, :, None], 1.0, jnp.sqrt(-jnp.expm1(2.0 * log_a)))
    b = mult * i * xf
    b = b.at[:, 0].add(a[:, 0] * h0.astype(jnp.float32))

    def combine(left, right):
        a1, b1 = left
        a2, b2 = right
        return a1 * a2, a2 * b1 + b2

    _, h = lax.associative_scan(combine, (a, b), axis=1)
    return h, h[:, -1]


def hybrid_layer(x, positions, attn_fn, conv_prev, h0, w_in, w_out, sinks, conv_w, conv_b,
                 lru_w_a, lru_b_a, lru_w_x, lru_b_x, lru_lambda, w_up, w_down,
                 g_pre_mix, g_post_mix, g_pre_ffn, g_post_ffn):
    B, T = x.shape[:2]
    xn = rmsnorm(x, g_pre_mix)
    z = jnp.einsum('btd,de->bte', xn, w_in)
    q, k, v, u, g_branch, gate_a, gate_r = jnp.split(z, SPLITS, axis=-1)
    attn_out, k_state, v_state = attn_fn(q.reshape(B, T, N_HEADS, HEAD_DIM),
                                         k.reshape(B, T, N_KV_HEADS, HEAD_DIM),
                                         v.reshape(B, T, N_KV_HEADS, HEAD_DIM), sinks)
    uc, conv_state = causal_conv(u, conv_prev, conv_w, conv_b)
    h, h_last = rg_lru(uc, h0, positions, lru_w_a, lru_b_a, lru_w_x, lru_b_x, lru_lambda)
    rnn_out = jax.nn.gelu(g_branch.astype(jnp.float32)) * h
    merged = (jax.nn.sigmoid(gate_a.astype(jnp.float32)) * attn_out.astype(jnp.float32)
              + jax.nn.sigmoid(gate_r.astype(jnp.float32)) * rnn_out).astype(x.dtype)
    mix = jnp.einsum('btd,de->bte', merged, w_out)
    x = x + rmsnorm(mix, g_post_mix)
    hn = rmsnorm(x, g_pre_ffn)
    f = jnp.einsum('btf,fd->btd', jnp.square(jax.nn.relu(jnp.einsum('btd,df->btf', hn, w_up))), w_down)
    x = x + rmsnorm(f, g_post_ffn)
    return x, k_state, v_state, conv_state, h_last


def setup_inputs(seed: int = 0) -> dict:
    key = jax.random.key(seed)
    ks = jax.random.split(key, 24)
    f32 = jnp.float32
    nrm = lambda k, shape, scale: jax.random.normal(k, shape, f32) * scale
    a0 = jax.random.uniform(ks[14], (DEPTH, D_RNN), f32, 0.9, 0.999)
    return {
        'x_prompt': nrm(ks[0], (BATCH, SEQ, D_MODEL), 1.0),
        'x_sample': nrm(ks[1], (DEC_BATCH, DEC_SEQ, D_MODEL), 1.0),
        'cache_k_win': nrm(ks[2], (DEPTH, DEC_BATCH, WINDOW, N_KV_HEADS, HEAD_DIM), 1.0),
        'cache_v_win': nrm(ks[3], (DEPTH, DEC_BATCH, WINDOW, N_KV_HEADS, HEAD_DIM), 1.0),
        'state_conv': nrm(ks[4], (DEPTH, DEC_BATCH, CONV_WIDTH - 1, D_RNN), 1.0),
        'state_lru': nrm(ks[5], (DEPTH, DEC_BATCH, D_RNN), 0.5),
        'w_in': nrm(ks[6], (DEPTH, D_MODEL, D_IN), D_MODEL ** -0.5),
        'w_out': nrm(ks[7], (DEPTH, D_MODEL, D_MODEL), D_MODEL ** -0.5),
        'sinks': nrm(ks[8], (DEPTH, N_HEADS), 0.5),
        'conv_w': nrm(ks[9], (DEPTH, CONV_WIDTH, D_RNN), CONV_WIDTH ** -0.5),
        'conv_b': nrm(ks[10], (DEPTH, D_RNN), 0.02),
        'lru_w_a': nrm(ks[11], (DEPTH, N_RNN_BLOCKS, RNN_BLOCK, RNN_BLOCK), RNN_BLOCK ** -0.5),
        'lru_b_a': nrm(ks[12], (DEPTH, D_RNN), 0.02),
        'lru_w_x': nrm(ks[13], (DEPTH, N_RNN_BLOCKS, RNN_BLOCK, RNN_BLOCK), RNN_BLOCK ** -0.5),
        'lru_b_x': nrm(ks[15], (DEPTH, D_RNN), 0.02),
        'lru_lambda': jnp.log(a0) - jnp.log1p(-a0),
        'w_up': nrm(ks[16], (DEPTH, D_MODEL, D_FF), D_MODEL ** -0.5),
        'w_down': nrm(ks[17], (DEPTH, D_FF, D_MODEL), D_FF ** -0.5),
        'g_pre_mix': 1.0 + nrm(ks[18], (DEPTH, D_MODEL), 0.02),
        'g_post_mix': 1.0 + nrm(ks[19], (DEPTH, D_MODEL), 0.02),
        'g_pre_ffn': 1.0 + nrm(ks[20], (DEPTH, D_MODEL), 0.02),
        'g_post_ffn': 1.0 + nrm(ks[21], (DEPTH, D_MODEL), 0.02),
    }


def reference(x_prompt, x_sample, cache_k_win, cache_v_win, state_conv, state_lru,
              w_in, w_out, sinks, conv_w, conv_b, lru_w_a, lru_b_a, lru_w_x, lru_b_x, lru_lambda,
              w_up, w_down, g_pre_mix, g_post_mix, g_pre_ffn, g_post_ffn):
    B, T = x_prompt.shape[:2]
    DB, DT = x_sample.shape[:2]
    pos_prompt = jnp.arange(T)
    pos_sample = PAST_LEN + jnp.arange(DT)
    yp, ys = x_prompt, x_sample
    kp, vp, cp, hp, kq, vq, cq, hq = [], [], [], [], [], [], [], []
    for l in range(DEPTH):
        lp = (w_in[l], w_out[l], sinks[l], conv_w[l], conv_b[l], lru_w_a[l], lru_b_a[l],
              lru_w_x[l], lru_b_x[l], lru_lambda[l], w_up[l], w_down[l],
              g_pre_mix[l], g_post_mix[l], g_pre_ffn[l], g_post_ffn[l])
        yp, k_s, v_s, c_s, h_s = hybrid_layer(
            yp, pos_prompt, window_attention_prompt,
            jnp.zeros((B, CONV_WIDTH - 1, D_RNN), yp.dtype), jnp.zeros((B, D_RNN), jnp.float32), *lp)
        kp.append(k_s); vp.append(v_s); cp.append(c_s); hp.append(h_s)
        ys, k_s, v_s, c_s, h_s = hybrid_layer(
            ys, pos_sample, functools.partial(window_attention_sample, cache_k_win[l], cache_v_win[l]),
            state_conv[l], state_lru[l], *lp)
        kq.append(k_s); vq.append(v_s); cq.append(c_s); hq.append(h_s)
    k_win_prompt = jnp.stack(kp)
    v_win_prompt = jnp.stack(vp)
    conv_prompt = jnp.stack(cp)
    lru_prompt = jnp.stack(hp)
    k_win_sample = jnp.stack(kq)
    v_win_sample = jnp.stack(vq)
    conv_sample = jnp.stack(cq)
    lru_sample = jnp.stack(hq)
    return (yp, ys, k_win_prompt, v_win_prompt, conv_prompt, lru_prompt,
            k_win_sample, v_win_sample, conv_sample, lru_sample)
```

```python
import functools

import jax
import jax.numpy as jnp
from jax import lax
from jax.experimental import pallas as pl
from jax.experimental.pallas import tpu as pltpu

D_MODEL = 1024
N_HEADS = 16
N_KV_HEADS = 4
HEAD_DIM = 64
GROUP = N_HEADS // N_KV_HEADS
D_ATTN = N_HEADS * HEAD_DIM
D_KV = N_KV_HEADS * HEAD_DIM
WINDOW = 128
D_RNN = D_MODEL
N_RNN_BLOCKS = 16
RNN_BLOCK = D_RNN // N_RNN_BLOCKS
CONV_WIDTH = 4
LRU_C = 8.0
D_FF = 4 * D_MODEL
EPS = 1e-6

OFF_Q = 0
OFF_K = D_ATTN
OFF_V = D_ATTN + D_KV
OFF_U = D_ATTN + 2 * D_KV
OFF_G = OFF_U + D_RNN
OFF_GA = OFF_G + D_RNN
OFF_GR = OFF_GA + D_MODEL
D_IN = OFF_GR + D_MODEL

V7X_SUBLANES = 8
V7X_LANES = 128
V7X_MXU_DIM = 256
V7X_VMEM_LIMIT_BYTES = 56 * 1024 * 1024

TQ = WINDOW
GATE_CHUNK = V7X_MXU_DIM
N_GATE_CHUNKS = D_RNN // GATE_CHUNK
SAMPLE_BC = 8
FFN_TM = 512

_BF16 = jnp.bfloat16
_F32 = jnp.float32


def _dot(a, b):
    return jnp.dot(a, b, preferred_element_type=_F32)


def _dot_nt(a, b):
    return lax.dot_general(a, b, (((1,), (1,)), ((), ())), preferred_element_type=_F32)


def _rmsnorm(x, g):
    return x * lax.rsqrt(jnp.mean(x * x, axis=-1, keepdims=True) + EPS) * g


def _softplus(y):
    return jnp.maximum(y, 0.0) + jnp.log1p(jnp.exp(-jnp.abs(y)))


def _lane_block(shape):
    return lax.broadcasted_iota(jnp.int32, shape, len(shape) - 1) // HEAD_DIM


def _lru_gates(uc, wab_ref, ba, bx, lam):
    xb = uc.astype(_BF16)
    ra, rx = [], []
    for c in range(N_GATE_CHUNKS):
        gc = _dot(xb[:, c * GATE_CHUNK:(c + 1) * GATE_CHUNK], wab_ref[c])
        ra.append(gc[:, :GATE_CHUNK])
        rx.append(gc[:, GATE_CHUNK:])
    r = jax.nn.sigmoid(jnp.concatenate(ra, axis=1) + ba)
    i = jax.nn.sigmoid(jnp.concatenate(rx, axis=1) + bx)
    log_a = -LRU_C * r * _softplus(-lam)
    a = jnp.exp(log_a)
    mult = jnp.sqrt(-jnp.tanh(log_a) * (a * a + 1.0))
    return a, mult, i


def _replicate_head_blocks(m):
    lo, hi = m[:, :V7X_LANES], m[:, V7X_LANES:]
    rlo = pltpu.roll(lo, HEAD_DIM, axis=1)
    rhi = pltpu.roll(hi, HEAD_DIM, axis=1)
    first = lax.broadcasted_iota(jnp.int32, lo.shape, 1) < HEAD_DIM
    halves = (jnp.where(first, lo, rlo), jnp.where(first, rlo, lo),
              jnp.where(first, hi, rhi), jnp.where(first, rhi, hi))
    return [jnp.concatenate([b, b], axis=1) for b in halves]


def _mix_prompt_kernel(sinks_ref, x_ref, gpre_ref, gpost_ref, win_ref, wout_ref, convw_ref, convb_ref,
                       wab_ref, ba_ref, bx_ref, lam_ref,
                       x1_ref, kwin_ref, vwin_ref, convst_ref, lru_ref,
                       kprev_ref, vprev_ref, uext_ref, hcarry_ref, a_ref, b_ref, h_ref):
    t = pl.program_id(1)
    last_t = pl.num_programs(1) - 1
    rd = t % 2
    wr = 1 - rd
    base = V7X_SUBLANES

    @pl.when(t == 0)
    def _():
        for h in range(N_KV_HEADS):
            kprev_ref[rd * N_KV_HEADS + h] = jnp.zeros((TQ, D_KV), _BF16)
            vprev_ref[rd * N_KV_HEADS + h] = jnp.zeros((TQ, D_KV), _BF16)
        uext_ref[rd, 0:base, :] = jnp.zeros((base, D_RNN), _F32)
        hcarry_ref[rd] = jnp.zeros((V7X_SUBLANES, D_RNN), _F32)

    x = x_ref[0]
    xnb = _rmsnorm(x, gpre_ref[...]).astype(_BF16)

    qkv = _dot(xnb, win_ref[:, OFF_Q:OFF_U])
    k = qkv[:, OFF_K:OFF_V]
    v = qkv[:, OFF_V:OFF_U]

    @pl.when(t == last_t)
    def _():
        kwin_ref[0] = k
        vwin_ref[0] = v

    kcur = [a.astype(_BF16) for a in _replicate_head_blocks(k)]
    vcur = [a.astype(_BF16) for a in _replicate_head_blocks(v)]

    qi = lax.broadcasted_iota(jnp.int32, (TQ, 2 * TQ), 0)
    kj = lax.broadcasted_iota(jnp.int32, (TQ, 2 * TQ), 1)
    diff = qi + TQ - kj
    first_key = jnp.where(t > 0, 0, TQ)
    mask = (diff >= 0) & (diff <= WINDOW) & (kj >= first_key)
    blk = _lane_block((TQ, D_KV))
    scale = HEAD_DIM ** -0.5

    attn_chunks = []
    for h in range(N_KV_HEADS):
        qc = qkv[:, h * D_KV:(h + 1) * D_KV] * scale
        qm = jnp.concatenate([jnp.where(blk == g, qc, 0.0).astype(_BF16) for g in range(GROUP)], axis=0)
        kk = jnp.concatenate([kprev_ref[rd * N_KV_HEADS + h], kcur[h]], axis=0)
        vv = jnp.concatenate([vprev_ref[rd * N_KV_HEADS + h], vcur[h]], axis=0)
        s = _dot_nt(qm, kk)
        ps, invs = [], []
        for g in range(GROUP):
            sg = jnp.where(mask, s[g * TQ:(g + 1) * TQ], -jnp.inf)
            sink = sinks_ref[h * GROUP + g]
            m = jnp.maximum(jnp.max(sg, axis=-1, keepdims=True), sink)
            p = jnp.exp(sg - m)
            den = jnp.sum(p, axis=-1, keepdims=True) + jnp.exp(sink - m)
            invs.append(1.0 / den)
            ps.append(p.astype(_BF16))
        o = _dot(jnp.concatenate(ps, axis=0), vv)
        acc = jnp.where(blk == 0, o[0:TQ] * invs[0], 0.0)
        for g in range(1, GROUP):
            acc = acc + jnp.where(blk == g, o[g * TQ:(g + 1) * TQ] * invs[g], 0.0)
        attn_chunks.append(acc)
    attn = jnp.concatenate(attn_chunks, axis=1)

    for h in range(N_KV_HEADS):
        kprev_ref[wr * N_KV_HEADS + h] = kcur[h]
        vprev_ref[wr * N_KV_HEADS + h] = vcur[h]

    u = _dot(xnb, win_ref[:, OFF_U:OFF_G])
    uext_ref[rd, base:base + TQ, :] = u
    cw = convw_ref[...]
    cs = uext_ref[rd, base - 3:base - 3 + TQ, :] * cw[0:1]
    cs = cs + uext_ref[rd, base - 2:base - 2 + TQ, :] * cw[1:2]
    cs = cs + uext_ref[rd, base - 1:base - 1 + TQ, :] * cw[2:3]
    cs = cs + u * cw[3:4]
    uc = convb_ref[...] + cs

    @pl.when(t == last_t)
    def _():
        convst_ref[0] = uext_ref[rd, base + TQ - (CONV_WIDTH - 1):base + TQ, :]

    uext_ref[wr, 0:base, :] = u[TQ - base:TQ, :]

    a, mult, i = _lru_gates(uc, wab_ref, ba_ref[...], bx_ref[...], lam_ref[...])
    pos = t * TQ + lax.broadcasted_iota(jnp.int32, (TQ, D_RNN), 0)
    mult = jnp.where(pos == 0, 1.0, mult)
    a_ref[...] = a
    b_ref[...] = mult * i * uc

    sub = lax.broadcasted_iota(jnp.int32, (V7X_SUBLANES, D_RNN), 0)
    carry = hcarry_ref[rd]
    for j in range(TQ // V7X_SUBLANES):
        rows = slice(j * V7X_SUBLANES, (j + 1) * V7X_SUBLANES)
        aj = a_ref[rows, :]
        bj = b_ref[rows, :]
        for sft in (1, 2, 4):
            keep = sub >= sft
            bj = jnp.where(keep, aj * pltpu.roll(bj, sft, axis=0) + bj, bj)
            aj = jnp.where(keep, aj * pltpu.roll(aj, sft, axis=0), aj)
        hj = aj * carry + bj
        h_ref[rows, :] = hj
        carry = jnp.broadcast_to(hj[V7X_SUBLANES - 1:V7X_SUBLANES, :], (V7X_SUBLANES, D_RNN))
    hcarry_ref[wr] = carry

    @pl.when(t == last_t)
    def _():
        lru_ref[0] = carry[0:1, :]

    g_branch = _dot(xnb, win_ref[:, OFF_G:OFF_GA])
    rnn_out = jax.nn.gelu(g_branch) * h_ref[...]

    gate_a = _dot(xnb, win_ref[:, OFF_GA:OFF_GR])
    gate_r = _dot(xnb, win_ref[:, OFF_GR:D_IN])
    merged = (jax.nn.sigmoid(gate_a) * attn + jax.nn.sigmoid(gate_r) * rnn_out).astype(_BF16)
    mix = _dot(merged, wout_ref[...])
    x1_ref[0] = x + _rmsnorm(mix, gpost_ref[...])


def _const_spec(shape):
    zeros = (0,) * len(shape)
    return pl.BlockSpec(shape, lambda *_: zeros, pipeline_mode=pl.Buffered(1))


def _mix_prompt(x, sinks, gpre, gpost, win_b, wout_b, convw, convb, wab, ba, bx, lam):
    B, T, D = x.shape
    nt = T // TQ
    row = lambda a: a.reshape(1, -1)
    return pl.pallas_call(
        _mix_prompt_kernel,
        grid=(B, nt),
        in_specs=[
            pl.BlockSpec(memory_space=pltpu.SMEM),
            pl.BlockSpec((1, TQ, D), lambda b, t: (b, t, 0)),
            _const_spec((1, D)), _const_spec((1, D)),
            _const_spec((D, D_IN)), _const_spec((D, D)),
            _const_spec((CONV_WIDTH, D_RNN)), _const_spec((1, D_RNN)),
            _const_spec((N_GATE_CHUNKS, GATE_CHUNK, 2 * GATE_CHUNK)),
            _const_spec((1, D_RNN)), _const_spec((1, D_RNN)), _const_spec((1, D_RNN)),
        ],
        out_specs=[
            pl.BlockSpec((1, TQ, D), lambda b, t: (b, t, 0)),
            pl.BlockSpec((1, WINDOW, D_KV), lambda b, t: (b, 0, 0)),
            pl.BlockSpec((1, WINDOW, D_KV), lambda b, t: (b, 0, 0)),
            pl.BlockSpec((1, CONV_WIDTH - 1, D_RNN), lambda b, t: (b, 0, 0)),
            pl.BlockSpec((1, 1, D_RNN), lambda b, t: (b, 0, 0)),
        ],
        out_shape=[
            jax.ShapeDtypeStruct((B, T, D), _F32),
            jax.ShapeDtypeStruct((B, WINDOW, D_KV), _F32),
            jax.ShapeDtypeStruct((B, WINDOW, D_KV), _F32),
            jax.ShapeDtypeStruct((B, CONV_WIDTH - 1, D_RNN), _F32),
            jax.ShapeDtypeStruct((B, 1, D_RNN), _F32),
        ],
        scratch_shapes=[
            pltpu.VMEM((2 * N_KV_HEADS, TQ, D_KV), _BF16),
            pltpu.VMEM((2 * N_KV_HEADS, TQ, D_KV), _BF16),
            pltpu.VMEM((2, V7X_SUBLANES + TQ, D_RNN), _F32),
            pltpu.VMEM((2, V7X_SUBLANES, D_RNN), _F32),
            pltpu.VMEM((TQ, D_RNN), _F32), pltpu.VMEM((TQ, D_RNN), _F32), pltpu.VMEM((TQ, D_RNN), _F32),
        ],
        compiler_params=pltpu.CompilerParams(
            dimension_semantics=("arbitrary", "arbitrary"), vmem_limit_bytes=V7X_VMEM_LIMIT_BYTES),
        name="mix_prompt",
    )(sinks, x, row(gpre), row(gpost), win_b, wout_b, convw, row(convb), wab, row(ba), row(bx), row(lam))


def _mix_sample_kernel(x_ref, ck_ref, cv_ref, cst_ref, h0_ref, sinks_ref, gpre_ref, gpost_ref, win_ref, wout_ref,
                       convw_ref, convb_ref, wab_ref, ba_ref, bx_ref, lam_ref,
                       x1_ref, kwin_ref, vwin_ref, convst_ref, lru_ref,
                       qrow_ref, orow_ref, knew_ref, vnew_ref, rnn_ref, siga_ref):
    step = pl.program_id(0)
    last = pl.num_programs(0) - 1
    nb = x_ref.shape[0]
    blk = _lane_block((nb, D_KV))

    @pl.when(step == 0)
    def _():
        xnb = _rmsnorm(x_ref[...], gpre_ref[...]).astype(_BF16)
        qkv = _dot(xnb, win_ref[:, OFF_Q:OFF_U])
        knew_ref[...] = qkv[:, OFF_K:OFF_V]
        vnew_ref[...] = qkv[:, OFF_V:OFF_U]
        scale = HEAD_DIM ** -0.5
        for h in range(N_KV_HEADS):
            qc = qkv[:, h * D_KV:(h + 1) * D_KV] * scale
            for g in range(GROUP):
                shift = ((h - g) % GROUP) * HEAD_DIM
                moved = pltpu.roll(qc, shift, axis=1) if shift else qc
                hd = h * GROUP + g
                placed = jnp.where(blk == h, moved, 0.0)
                for half in range(D_KV // V7X_LANES):
                    qrow_ref[half, hd * nb:(hd + 1) * nb, :] = placed[:, half * V7X_LANES:(half + 1) * V7X_LANES]

        u = _dot(xnb, win_ref[:, OFF_U:OFF_G])
        cw = convw_ref[...]
        cs = cst_ref[0] * cw[0:1]
        cs = cs + cst_ref[1] * cw[1:2]
        cs = cs + cst_ref[2] * cw[2:3]
        cs = cs + u * cw[3:4]
        uc = convb_ref[...] + cs
        convst_ref[0] = cst_ref[1]
        convst_ref[1] = cst_ref[2]
        convst_ref[2] = u
        a, mult, i = _lru_gates(uc, wab_ref, ba_ref[...], bx_ref[...], lam_ref[...])
        hnew = a * h0_ref[...] + mult * i * uc
        lru_ref[...] = hnew
        g_branch = _dot(xnb, win_ref[:, OFF_G:OFF_GA])
        gate_r = _dot(xnb, win_ref[:, OFF_GR:D_IN])
        rnn_ref[...] = jax.nn.sigmoid(gate_r) * (jax.nn.gelu(g_branch) * hnew)
        siga_ref[...] = jax.nn.sigmoid(_dot(xnb, win_ref[:, OFF_GA:OFF_GR]))

    sink = sinks_ref[...]
    rowid = lax.broadcasted_iota(jnp.int32, (WINDOW, D_KV), 0)

    def per_row(bl, carry):
        b = step * SAMPLE_BC + bl
        kb = ck_ref[bl]
        vb = cv_ref[bl]
        kn = knew_ref[pl.ds(b, 1), :]
        vn = vnew_ref[pl.ds(b, 1), :]
        kwin_ref[bl] = jnp.where(rowid == WINDOW - 1, kn, pltpu.roll(kb, WINDOW - 1, axis=0))
        vwin_ref[bl] = jnp.where(rowid == WINDOW - 1, vn, pltpu.roll(vb, WINDOW - 1, axis=0))

        head_rows = pl.ds(b, N_HEADS, stride=nb)
        qm = jnp.concatenate([qrow_ref[half, head_rows, :] for half in range(D_KV // V7X_LANES)],
                             axis=1).astype(_BF16)
        s = _dot_nt(qm, kb.astype(_BF16))
        knr = kn.astype(_BF16).astype(_F32)
        s_self = jnp.sum(qm.astype(_F32) * knr, axis=-1, keepdims=True)
        m = jnp.maximum(jnp.maximum(jnp.max(s, axis=-1, keepdims=True), s_self), sink)
        p = jnp.exp(s - m)
        p_self = jnp.exp(s_self - m)
        den = jnp.sum(p, axis=-1, keepdims=True) + p_self + jnp.exp(sink - m)
        o = _dot(p.astype(_BF16), vb.astype(_BF16))
        o = o + p_self.astype(_BF16).astype(_F32) * vn.astype(_BF16).astype(_F32)
        o = o * (1.0 / den)
        for half in range(D_KV // V7X_LANES):
            orow_ref[half, head_rows, :] = o[:, half * V7X_LANES:(half + 1) * V7X_LANES]
        return carry

    lax.fori_loop(0, SAMPLE_BC, per_row, 0)

    @pl.when(step == last)
    def _():
        chunks = []
        for h in range(N_KV_HEADS):
            acc = None
            for g in range(GROUP):
                hd = h * GROUP + g
                rows = jnp.concatenate([orow_ref[half, hd * nb:(hd + 1) * nb, :]
                                        for half in range(D_KV // V7X_LANES)], axis=1)
                sel = jnp.where(blk == h, rows, 0.0)
                shift = ((g - h) % GROUP) * HEAD_DIM
                moved = pltpu.roll(sel, shift, axis=1) if shift else sel
                acc = moved if acc is None else acc + moved
            chunks.append(acc)
        attn = jnp.concatenate(chunks, axis=1)
        merged = (siga_ref[...] * attn + rnn_ref[...]).astype(_BF16)
        mix = _dot(merged, wout_ref[...])
        x1_ref[...] = x_ref[...] + _rmsnorm(mix, gpost_ref[...])


def _mix_sample(x, ck, cv, cst, h0, sinks, gpre, gpost, win_b, wout_b, convw, convb, wab, ba, bx, lam):
    nb, D = x.shape
    row = lambda a: a.reshape(1, -1)
    cache_spec = pl.BlockSpec((SAMPLE_BC, WINDOW, D_KV), lambda s: (s, 0, 0))
    return pl.pallas_call(
        _mix_sample_kernel,
        grid=(nb // SAMPLE_BC,),
        in_specs=[
            _const_spec((nb, D)), cache_spec, cache_spec,
            _const_spec((CONV_WIDTH - 1, nb, D_RNN)), _const_spec((nb, D_RNN)),
            _const_spec((N_HEADS, 1)),
            _const_spec((1, D)), _const_spec((1, D)),
            _const_spec((D, D_IN)), _const_spec((D, D)),
            _const_spec((CONV_WIDTH, D_RNN)), _const_spec((1, D_RNN)),
            _const_spec((N_GATE_CHUNKS, GATE_CHUNK, 2 * GATE_CHUNK)),
            _const_spec((1, D_RNN)), _const_spec((1, D_RNN)), _const_spec((1, D_RNN)),
        ],
        out_specs=[
            pl.BlockSpec((nb, D), lambda s: (0, 0)),
            cache_spec, cache_spec,
            pl.BlockSpec((CONV_WIDTH - 1, nb, D_RNN), lambda s: (0, 0, 0)),
            pl.BlockSpec((nb, D_RNN), lambda s: (0, 0)),
        ],
        out_shape=[
            jax.ShapeDtypeStruct((nb, D), _F32),
            jax.ShapeDtypeStruct((nb, WINDOW, D_KV), _F32),
            jax.ShapeDtypeStruct((nb, WINDOW, D_KV), _F32),
            jax.ShapeDtypeStruct((CONV_WIDTH - 1, nb, D_RNN), _F32),
            jax.ShapeDtypeStruct((nb, D_RNN), _F32),
        ],
        scratch_shapes=[
            pltpu.VMEM((D_KV // V7X_LANES, N_HEADS * nb, V7X_LANES), _F32),
            pltpu.VMEM((D_KV // V7X_LANES, N_HEADS * nb, V7X_LANES), _F32),
            pltpu.VMEM((nb, D_KV), _F32), pltpu.VMEM((nb, D_KV), _F32),
            pltpu.VMEM((nb, D_RNN), _F32), pltpu.VMEM((nb, D), _F32),
        ],
        compiler_params=pltpu.CompilerParams(
            dimension_semantics=("arbitrary",), vmem_limit_bytes=V7X_VMEM_LIMIT_BYTES),
        name="mix_sample",
    )(x, ck, cv, cst, h0, sinks.reshape(N_HEADS, 1), row(gpre), row(gpost), win_b, wout_b, convw, row(convb),
      wab, row(ba), row(bx), row(lam))


def _ffn_kernel(x_ref, gpre_ref, gpost_ref, wup_ref, wdn_ref, o_ref):
    x = x_ref[...]
    hn = _rmsnorm(x, gpre_ref[...]).astype(_BF16)
    up = _dot(hn, wup_ref[...])
    act = jnp.square(jnp.maximum(up, 0.0)).astype(_BF16)
    f = _dot(act, wdn_ref[...])
    o_ref[...] = x + _rmsnorm(f, gpost_ref[...])


def _ffn(x, gpre, gpost, wup_b, wdn_b, tm):
    M, D = x.shape
    row = lambda a: a.reshape(1, -1)
    return pl.pallas_call(
        _ffn_kernel,
        grid=(M // tm,),
        in_specs=[
            pl.BlockSpec((tm, D), lambda i: (i, 0)),
            _const_spec((1, D)), _const_spec((1, D)),
            _const_spec((D, D_FF)), _const_spec((D_FF, D)),
        ],
        out_specs=pl.BlockSpec((tm, D), lambda i: (i, 0)),
        out_shape=jax.ShapeDtypeStruct((M, D), _F32),
        compiler_params=pltpu.CompilerParams(
            dimension_semantics=("arbitrary",), vmem_limit_bytes=V7X_VMEM_LIMIT_BYTES),
        name="ffn",
    )(x, row(gpre), row(gpost), wup_b, wdn_b)


def _gate_weights(w_a, w_x):
    per = GATE_CHUNK // RNN_BLOCK
    eye = jnp.eye(per, dtype=bool)[None, :, None, :, None]

    def bd(w):
        w = w.reshape(N_GATE_CHUNKS, per, RNN_BLOCK, 1, RNN_BLOCK)
        return jnp.where(eye, w, 0.0).reshape(N_GATE_CHUNKS, GATE_CHUNK, GATE_CHUNK)

    return jnp.concatenate([bd(w_a), bd(w_x)], axis=-1).astype(_BF16)


def kernel(x_prompt, x_sample, cache_k_win, cache_v_win, state_conv, state_lru, w_in, w_out, sinks, conv_w, conv_b,
           lru_w_a, lru_b_a, lru_w_x, lru_b_x, lru_lambda, w_up, w_down, g_pre_mix, g_post_mix, g_pre_ffn,
           g_post_ffn):
    B, T, D = x_prompt.shape
    DB, DT, _ = x_sample.shape
    depth = w_in.shape[0]
    assert depth == 1 and DT == 1 and T % TQ == 0 and DB % SAMPLE_BC == 0 and (B * T) % FFN_TM == 0
    l = 0
    win_b = w_in[l].astype(_BF16)
    wout_b = w_out[l].astype(_BF16)
    wup_b = w_up[l].astype(_BF16)
    wdn_b = w_down[l].astype(_BF16)
    wab = _gate_weights(lru_w_a[l], lru_w_x[l])
    layer = (g_pre_mix[l], g_post_mix[l], win_b, wout_b, conv_w[l], conv_b[l], wab, lru_b_a[l], lru_b_x[l],
             lru_lambda[l])

    x1p, kp, vp, cp, hp = _mix_prompt(x_prompt, sinks[l], *layer)
    yp = _ffn(x1p.reshape(B * T, D), g_pre_ffn[l], g_post_ffn[l], wup_b, wdn_b, FFN_TM).reshape(B, T, D)

    x1s, kq, vq, cq, hq = _mix_sample(
        x_sample.reshape(DB, D), cache_k_win[l].reshape(DB, WINDOW, D_KV), cache_v_win[l].reshape(DB, WINDOW, D_KV),
        jnp.swapaxes(state_conv[l], 0, 1), state_lru[l], sinks[l], *layer)
    ys = _ffn(x1s, g_pre_ffn[l], g_post_ffn[l], wup_b, wdn_b, DB).reshape(DB, DT, D)

    kv_shape = (1, -1, WINDOW, N_KV_HEADS, HEAD_DIM)
    return (yp, ys, kp.reshape(kv_shape), vp.reshape(kv_shape), cp[None], hp.reshape(1, B, D_RNN),
            kq.reshape(kv_shape), vq.reshape(kv_shape), jnp.swapaxes(cq, 0, 1)[None], hq[None])
```

```python
import functools

import jax
import jax.numpy as jnp
from jax import lax
from jax.experimental import pallas as pl
from jax.experimental.pallas import tpu as pltpu

D_MODEL = 1024
N_HEADS = 16
N_KV_HEADS = 4
HEAD_DIM = 64
GROUP = N_HEADS // N_KV_HEADS
D_ATTN = N_HEADS * HEAD_DIM
D_KV = N_KV_HEADS * HEAD_DIM
WINDOW = 128
D_RNN = D_MODEL
N_RNN_BLOCKS = 16
RNN_BLOCK = D_RNN // N_RNN_BLOCKS
CONV_WIDTH = 4
LRU_C = 8.0
D_FF = 4 * D_MODEL
EPS = 1e-6

OFF_Q = 0
OFF_K = D_ATTN
OFF_V = D_ATTN + D_KV
OFF_U = D_ATTN + 2 * D_KV
OFF_G = OFF_U + D_RNN
OFF_GA = OFF_G + D_RNN
OFF_GR = OFF_GA + D_MODEL
D_IN = OFF_GR + D_MODEL

V7X_SUBLANES = 8
V7X_LANES = 128
V7X_MXU_DIM = 256
V7X_VMEM_LIMIT_BYTES = 56 * 1024 * 1024

TQ = WINDOW
GATE_CHUNK = V7X_MXU_DIM
N_GATE_CHUNKS = D_RNN // GATE_CHUNK
SAMPLE_BC = 8
FFN_TM = 512

_BF16 = jnp.bfloat16
_F32 = jnp.float32


def _dot(a, b):
    return jnp.dot(a, b, preferred_element_type=_F32)


def _dot_nt(a, b):
    return lax.dot_general(a, b, (((1,), (1,)), ((), ())), preferred_element_type=_F32)


def _rmsnorm(x, g):
    return x * lax.rsqrt(jnp.mean(x * x, axis=-1, keepdims=True) + EPS) * g


def _softplus(y):
    return jnp.maximum(y, 0.0) + jnp.log1p(jnp.exp(-jnp.abs(y)))


def _lane_block(shape):
    return lax.broadcasted_iota(jnp.int32, shape, len(shape) - 1) // HEAD_DIM


def _lru_gates(uc, wab_ref, ba, bx, lam):
    xb = uc.astype(_BF16)
    ra, rx = [], []
    for c in range(N_GATE_CHUNKS):
        gc = _dot(xb[:, c * GATE_CHUNK:(c + 1) * GATE_CHUNK], wab_ref[c])
        ra.append(gc[:, :GATE_CHUNK])
        rx.append(gc[:, GATE_CHUNK:])
    r = jax.nn.sigmoid(jnp.concatenate(ra, axis=1) + ba)
    i = jax.nn.sigmoid(jnp.concatenate(rx, axis=1) + bx)
    log_a = -LRU_C * r * _softplus(-lam)
    a = jnp.exp(log_a)
    mult = jnp.sqrt(-jnp.tanh(log_a) * (a * a + 1.0))
    return a, mult, i


def _replicate_head_blocks(m):
    lo, hi = m[:, :V7X_LANES], m[:, V7X_LANES:]
    rlo = pltpu.roll(lo, HEAD_DIM, axis=1)
    rhi = pltpu.roll(hi, HEAD_DIM, axis=1)
    first = lax.broadcasted_iota(jnp.int32, lo.shape, 1) < HEAD_DIM
    halves = (jnp.where(first, lo, rlo), jnp.where(first, rlo, lo),
              jnp.where(first, hi, rhi), jnp.where(first, rhi, hi))
    return [jnp.concatenate([b, b], axis=1) for b in halves]


def _mix_prompt_kernel(*refs, nt):
    s = pl.program_id(0)
    *_, z0_ref, z1_ref, kprev_ref, vprev_ref, uext_ref, hcarry_ref, _a_ref, _b_ref, _h_ref = refs

    @pl.when(s == 0)
    def _():
        for ref in (z0_ref, z1_ref, kprev_ref, vprev_ref, uext_ref, hcarry_ref):
            ref[...] = jnp.zeros(ref.shape, ref.dtype)

    for par in (0, 1):
        pl.when(s % 2 == par)(functools.partial(_mix_prompt_step, par, s, *refs, nt=nt))


def _mix_prompt_step(par, s, sinks_ref, xa_ref, xb_ref, gpre_ref, gpost_ref, win_ref, wout_ref, convw_ref, convb_ref,
                     wab_ref, ba_ref, bx_ref, lam_ref,
                     x1_ref, kwin_ref, vwin_ref, convst_ref, lru_ref,
                     z0_ref, z1_ref, kprev_ref, vprev_ref, uext_ref, hcarry_ref, a_ref, b_ref, h_ref, *, nt):
    za_ref, zb_ref = (z0_ref, z1_ref) if par == 0 else (z1_ref, z0_ref)
    xnb = _rmsnorm(xa_ref[0], gpre_ref[...]).astype(_BF16)
    a_tiles = iter(range(D_IN // V7X_MXU_DIM))

    def stage_a(n):
        for _ in range(n):
            c = next(a_tiles) * V7X_MXU_DIM
            za_ref[:, c:c + V7X_MXU_DIM] = _dot(xnb, win_ref[:, c:c + V7X_MXU_DIM])

    t = jnp.maximum(s - 1, 0) % nt
    assert nt % 2 == 0
    rd, wr = 1 - par, par
    base = V7X_SUBLANES


    k = zb_ref[:, OFF_K:OFF_V]
    v = zb_ref[:, OFF_V:OFF_U]
    kwin_ref[0] = k
    vwin_ref[0] = v

    kcur = [a.astype(_BF16) for a in _replicate_head_blocks(k)]
    vcur = [a.astype(_BF16) for a in _replicate_head_blocks(v)]

    qi = lax.broadcasted_iota(jnp.int32, (TQ, 2 * TQ), 0)
    kj = lax.broadcasted_iota(jnp.int32, (TQ, 2 * TQ), 1)
    diff = qi + TQ - kj
    first_key = jnp.where(t > 0, 0, TQ)
    mask = (diff >= 0) & (diff <= WINDOW) & (kj >= first_key)
    blk = _lane_block((TQ, D_KV))
    scale = HEAD_DIM ** -0.5

    attn_chunks = []
    for h in range(N_KV_HEADS):
        qc = zb_ref[:, h * D_KV:(h + 1) * D_KV] * scale
        qm = jnp.concatenate([jnp.where(blk == g, qc, 0.0).astype(_BF16) for g in range(GROUP)], axis=0)
        kk = jnp.concatenate([kprev_ref[rd * N_KV_HEADS + h], kcur[h]], axis=0)
        vv = jnp.concatenate([vprev_ref[rd * N_KV_HEADS + h], vcur[h]], axis=0)
        sc = _dot_nt(qm, kk)
        ps, invs = [], []
        for g in range(GROUP):
            sg = jnp.where(mask, sc[g * TQ:(g + 1) * TQ], -jnp.inf)
            sink = sinks_ref[h * GROUP + g]
            m = jnp.maximum(jnp.max(sg, axis=-1, keepdims=True), sink)
            p = jnp.exp(sg - m)
            den = jnp.sum(p, axis=-1, keepdims=True) + jnp.exp(sink - m)
            invs.append(1.0 / den)
            ps.append(p.astype(_BF16))
            if g == GROUP - 1:
                stage_a(1)
        o = _dot(jnp.concatenate(ps, axis=0), vv)
        acc = jnp.where(blk == 0, o[0:TQ] * invs[0], 0.0)
        for g in range(1, GROUP):
            acc = acc + jnp.where(blk == g, o[g * TQ:(g + 1) * TQ] * invs[g], 0.0)
        attn_chunks.append(acc)
    attn = jnp.concatenate(attn_chunks, axis=1)

    for h in range(N_KV_HEADS):
        kprev_ref[wr * N_KV_HEADS + h] = kcur[h]
        vprev_ref[wr * N_KV_HEADS + h] = vcur[h]

    u = zb_ref[:, OFF_U:OFF_G]
    uext_ref[rd, 0:base, :] = jnp.where(t > 0, uext_ref[rd, 0:base, :], 0.0)
    uext_ref[rd, base:base + TQ, :] = u
    cw = convw_ref[...]
    cs = uext_ref[rd, base - 3:base - 3 + TQ, :] * cw[0:1]
    cs = cs + uext_ref[rd, base - 2:base - 2 + TQ, :] * cw[1:2]
    cs = cs + uext_ref[rd, base - 1:base - 1 + TQ, :] * cw[2:3]
    cs = cs + u * cw[3:4]
    uc = convb_ref[...] + cs
    stage_a(3)

    convst_ref[0] = uext_ref[rd, base + TQ - (CONV_WIDTH - 1):base + TQ, :]

    uext_ref[wr, 0:base, :] = u[TQ - base:TQ, :]

    a, mult, i = _lru_gates(uc, wab_ref, ba_ref[...], bx_ref[...], lam_ref[...])
    stage_a(5)
    pos = t * TQ + lax.broadcasted_iota(jnp.int32, (TQ, D_RNN), 0)
    mult = jnp.where(pos == 0, 1.0, mult)
    a_ref[...] = a
    b_ref[...] = mult * i * uc

    sub = lax.broadcasted_iota(jnp.int32, (V7X_SUBLANES, D_RNN), 0)
    carry = jnp.where(t > 0, hcarry_ref[rd], 0.0)
    for j in range(TQ // V7X_SUBLANES):
        rows = slice(j * V7X_SUBLANES, (j + 1) * V7X_SUBLANES)
        aj = a_ref[rows, :]
        bj = b_ref[rows, :]
        for sft in (1, 2, 4):
            keep = sub >= sft
            bj = jnp.where(keep, aj * pltpu.roll(bj, sft, axis=0) + bj, bj)
            aj = jnp.where(keep, aj * pltpu.roll(aj, sft, axis=0), aj)
        hj = aj * carry + bj
        h_ref[rows, :] = hj
        if j % 2 == 1:
            stage_a(1)
        carry = jnp.broadcast_to(hj[V7X_SUBLANES - 1:V7X_SUBLANES, :], (V7X_SUBLANES, D_RNN))
    hcarry_ref[wr] = carry

    lru_ref[0] = carry[0:1, :]

    rnn_out = jax.nn.gelu(zb_ref[:, OFF_G:OFF_GA]) * h_ref[...]

    gate_a = zb_ref[:, OFF_GA:OFF_GR]
    gate_r = zb_ref[:, OFF_GR:D_IN]
    stage_a(1)
    merged = (jax.nn.sigmoid(gate_a) * attn + jax.nn.sigmoid(gate_r) * rnn_out).astype(_BF16)
    stage_a(1)
    assert next(a_tiles, None) is None
    mix = _dot(merged, wout_ref[...])
    x1_ref[0] = xb_ref[0] + _rmsnorm(mix, gpost_ref[...])


def _const_spec(shape):
    zeros = (0,) * len(shape)
    return pl.BlockSpec(shape, lambda *_: zeros, pipeline_mode=pl.Buffered(1))


def _mix_prompt(x, sinks, gpre, gpost, win_b, wout_b, convw, convb, wab, ba, bx, lam):
    B, T, D = x.shape
    nt = T // TQ
    n_blocks = B * nt
    row = lambda a: a.reshape(1, -1)
    cur = lambda s: jnp.minimum(s, n_blocks - 1)
    prv = lambda s: jnp.maximum(s - 1, 0)
    return pl.pallas_call(
        functools.partial(_mix_prompt_kernel, nt=nt),
        grid=(n_blocks + 1,),
        in_specs=[
            pl.BlockSpec(memory_space=pltpu.SMEM),
            pl.BlockSpec((1, TQ, D), lambda s: (cur(s) // nt, cur(s) % nt, 0)),
            pl.BlockSpec((1, TQ, D), lambda s: (prv(s) // nt, prv(s) % nt, 0)),
            _const_spec((1, D)), _const_spec((1, D)),
            _const_spec((D, D_IN)), _const_spec((D, D)),
            _const_spec((CONV_WIDTH, D_RNN)), _const_spec((1, D_RNN)),
            _const_spec((N_GATE_CHUNKS, GATE_CHUNK, 2 * GATE_CHUNK)),
            _const_spec((1, D_RNN)), _const_spec((1, D_RNN)), _const_spec((1, D_RNN)),
        ],
        out_specs=[
            pl.BlockSpec((1, TQ, D), lambda s: (prv(s) // nt, prv(s) % nt, 0)),
            pl.BlockSpec((1, WINDOW, D_KV), lambda s: (prv(s) // nt, 0, 0)),
            pl.BlockSpec((1, WINDOW, D_KV), lambda s: (prv(s) // nt, 0, 0)),
            pl.BlockSpec((1, CONV_WIDTH - 1, D_RNN), lambda s: (prv(s) // nt, 0, 0)),
            pl.BlockSpec((1, 1, D_RNN), lambda s: (prv(s) // nt, 0, 0)),
        ],
        out_shape=[
            jax.ShapeDtypeStruct((B, T, D), _F32),
            jax.ShapeDtypeStruct((B, WINDOW, D_KV), _F32),
            jax.ShapeDtypeStruct((B, WINDOW, D_KV), _F32),
            jax.ShapeDtypeStruct((B, CONV_WIDTH - 1, D_RNN), _F32),
            jax.ShapeDtypeStruct((B, 1, D_RNN), _F32),
        ],
        scratch_shapes=[
            pltpu.VMEM((TQ, D_IN), _F32), pltpu.VMEM((TQ, D_IN), _F32),
            pltpu.VMEM((2 * N_KV_HEADS, TQ, D_KV), _BF16),
            pltpu.VMEM((2 * N_KV_HEADS, TQ, D_KV), _BF16),
            pltpu.VMEM((2, V7X_SUBLANES + TQ, D_RNN), _F32),
            pltpu.VMEM((2, V7X_SUBLANES, D_RNN), _F32),
            pltpu.VMEM((TQ, D_RNN), _F32), pltpu.VMEM((TQ, D_RNN), _F32), pltpu.VMEM((TQ, D_RNN), _F32),
        ],
        compiler_params=pltpu.CompilerParams(
            dimension_semantics=("arbitrary",), vmem_limit_bytes=V7X_VMEM_LIMIT_BYTES),
        name="mix_prompt",
    )(sinks, x, x, row(gpre), row(gpost), win_b, wout_b, convw, row(convb), wab, row(ba), row(bx), row(lam))


def _mix_sample_kernel(x_ref, ck_ref, cv_ref, cst_ref, h0_ref, sinks_ref, gpre_ref, gpost_ref, win_ref, wout_ref,
                       convw_ref, convb_ref, wab_ref, ba_ref, bx_ref, lam_ref,
                       x1_ref, kwin_ref, vwin_ref, convst_ref, lru_ref,
                       qrow_ref, orow_ref, knew_ref, vnew_ref, rnn_ref, siga_ref):
    step = pl.program_id(0)
    last = pl.num_programs(0) - 1
    nb = x_ref.shape[0]
    blk = _lane_block((nb, D_KV))

    @pl.when(step == 0)
    def _():
        xnb = _rmsnorm(x_ref[...], gpre_ref[...]).astype(_BF16)
        qkv = _dot(xnb, win_ref[:, OFF_Q:OFF_U])
        knew_ref[...] = qkv[:, OFF_K:OFF_V]
        vnew_ref[...] = qkv[:, OFF_V:OFF_U]
        scale = HEAD_DIM ** -0.5
        for h in range(N_KV_HEADS):
            qc = qkv[:, h * D_KV:(h + 1) * D_KV] * scale
            for g in range(GROUP):
                shift = ((h - g) % GROUP) * HEAD_DIM
                moved = pltpu.roll(qc, shift, axis=1) if shift else qc
                hd = h * GROUP + g
                placed = jnp.where(blk == h, moved, 0.0)
                for half in range(D_KV // V7X_LANES):
                    qrow_ref[half, hd * nb:(hd + 1) * nb, :] = placed[:, half * V7X_LANES:(half + 1) * V7X_LANES]

        u = _dot(xnb, win_ref[:, OFF_U:OFF_G])
        cw = convw_ref[...]
        cs = cst_ref[0] * cw[0:1]
        cs = cs + cst_ref[1] * cw[1:2]
        cs = cs + cst_ref[2] * cw[2:3]
        cs = cs + u * cw[3:4]
        uc = convb_ref[...] + cs
        convst_ref[0] = cst_ref[1]
        convst_ref[1] = cst_ref[2]
        convst_ref[2] = u
        a, mult, i = _lru_gates(uc, wab_ref, ba_ref[...], bx_ref[...], lam_ref[...])
        hnew = a * h0_ref[...] + mult * i * uc
        lru_ref[...] = hnew
        g_branch = _dot(xnb, win_ref[:, OFF_G:OFF_GA])
        gate_r = _dot(xnb, win_ref[:, OFF_GR:D_IN])
        rnn_ref[...] = jax.nn.sigmoid(gate_r) * (jax.nn.gelu(g_branch) * hnew)
        siga_ref[...] = jax.nn.sigmoid(_dot(xnb, win_ref[:, OFF_GA:OFF_GR]))

    sink = sinks_ref[...]
    rowid = lax.broadcasted_iota(jnp.int32, (WINDOW, D_KV), 0)

    def per_row(bl, carry):
        b = step * SAMPLE_BC + bl
        kb = ck_ref[bl]
        vb = cv_ref[bl]
        kn = knew_ref[pl.ds(b, 1), :]
        vn = vnew_ref[pl.ds(b, 1), :]
        kwin_ref[bl] = jnp.where(rowid == WINDOW - 1, kn, pltpu.roll(kb, WINDOW - 1, axis=0))
        vwin_ref[bl] = jnp.where(rowid == WINDOW - 1, vn, pltpu.roll(vb, WINDOW - 1, axis=0))

        head_rows = pl.ds(b, N_HEADS, stride=nb)
        qm = jnp.concatenate([qrow_ref[half, head_rows, :] for half in range(D_KV // V7X_LANES)],
                             axis=1).astype(_BF16)
        s = _dot_nt(qm, kb.astype(_BF16))
        knr = kn.astype(_BF16).astype(_F32)
        s_self = jnp.sum(qm.astype(_F32) * knr, axis=-1, keepdims=True)
        m = jnp.maximum(jnp.maximum(jnp.max(s, axis=-1, keepdims=True), s_self), sink)
        p = jnp.exp(s - m)
        p_self = jnp.exp(s_self - m)
        den = jnp.sum(p, axis=-1, keepdims=True) + p_self + jnp.exp(sink - m)
        o = _dot(p.astype(_BF16), vb.astype(_BF16))
        o = o + p_self.astype(_BF16).astype(_F32) * vn.astype(_BF16).astype(_F32)
        o = o * (1.0 / den)
        for half in range(D_KV // V7X_LANES):
            orow_ref[half, head_rows, :] = o[:, half * V7X_LANES:(half + 1) * V7X_LANES]
        return carry

    lax.fori_loop(0, SAMPLE_BC, per_row, 0)

    @pl.when(step == last)
    def _():
        chunks = []
        for h in range(N_KV_HEADS):
            acc = None
            for g in range(GROUP):
                hd = h * GROUP + g
                rows = jnp.concatenate([orow_ref[half, hd * nb:(hd + 1) * nb, :]
                                        for half in range(D_KV // V7X_LANES)], axis=1)
                sel = jnp.where(blk == h, rows, 0.0)
                shift = ((g - h) % GROUP) * HEAD_DIM
                moved = pltpu.roll(sel, shift, axis=1) if shift else sel
                acc = moved if acc is None else acc + moved
            chunks.append(acc)
        attn = jnp.concatenate(chunks, axis=1)
        merged = (siga_ref[...] * attn + rnn_ref[...]).astype(_BF16)
        mix = _dot(merged, wout_ref[...])
        x1_ref[...] = x_ref[...] + _rmsnorm(mix, gpost_ref[...])


def _mix_sample(x, ck, cv, cst, h0, sinks, gpre, gpost, win_b, wout_b, convw, convb, wab, ba, bx, lam):
    nb, D = x.shape
    row = lambda a: a.reshape(1, -1)
    cache_spec = pl.BlockSpec((SAMPLE_BC, WINDOW, D_KV), lambda s: (s, 0, 0))
    return pl.pallas_call(
        _mix_sample_kernel,
        grid=(nb // SAMPLE_BC,),
        in_specs=[
            _const_spec((nb, D)), cache_spec, cache_spec,
            _const_spec((CONV_WIDTH - 1, nb, D_RNN)), _const_spec((nb, D_RNN)),
            _const_spec((N_HEADS, 1)),
            _const_spec((1, D)), _const_spec((1, D)),
            _const_spec((D, D_IN)), _const_spec((D, D)),
            _const_spec((CONV_WIDTH, D_RNN)), _const_spec((1, D_RNN)),
            _const_spec((N_GATE_CHUNKS, GATE_CHUNK, 2 * GATE_CHUNK)),
            _const_spec((1, D_RNN)), _const_spec((1, D_RNN)), _const_spec((1, D_RNN)),
        ],
        out_specs=[
            pl.BlockSpec((nb, D), lambda s: (0, 0)),
            cache_spec, cache_spec,
            pl.BlockSpec((CONV_WIDTH - 1, nb, D_RNN), lambda s: (0, 0, 0)),
            pl.BlockSpec((nb, D_RNN), lambda s: (0, 0)),
        ],
        out_shape=[
            jax.ShapeDtypeStruct((nb, D), _F32),
            jax.ShapeDtypeStruct((nb, WINDOW, D_KV), _F32),
            jax.ShapeDtypeStruct((nb, WINDOW, D_KV), _F32),
            jax.ShapeDtypeStruct((CONV_WIDTH - 1, nb, D_RNN), _F32),
            jax.ShapeDtypeStruct((nb, D_RNN), _F32),
        ],
        scratch_shapes=[
            pltpu.VMEM((D_KV // V7X_LANES, N_HEADS * nb, V7X_LANES), _F32),
            pltpu.VMEM((D_KV // V7X_LANES, N_HEADS * nb, V7X_LANES), _F32),
            pltpu.VMEM((nb, D_KV), _F32), pltpu.VMEM((nb, D_KV), _F32),
            pltpu.VMEM((nb, D_RNN), _F32), pltpu.VMEM((nb, D), _F32),
        ],
        compiler_params=pltpu.CompilerParams(
            dimension_semantics=("arbitrary",), vmem_limit_bytes=V7X_VMEM_LIMIT_BYTES),
        name="mix_sample",
    )(x, ck, cv, cst, h0, sinks.reshape(N_HEADS, 1), row(gpre), row(gpost), win_b, wout_b, convw, row(convb),
      wab, row(ba), row(bx), row(lam))


def _ffn_kernel(x_ref, gpre_ref, gpost_ref, wup_ref, wdn_ref, o_ref):
    x = x_ref[...]
    hn = _rmsnorm(x, gpre_ref[...]).astype(_BF16)
    up = _dot(hn, wup_ref[...])
    act = jnp.square(jnp.maximum(up, 0.0)).astype(_BF16)
    f = _dot(act, wdn_ref[...])
    o_ref[...] = x + _rmsnorm(f, gpost_ref[...])


def _ffn(x, gpre, gpost, wup_b, wdn_b, tm):
    M, D = x.shape
    row = lambda a: a.reshape(1, -1)
    return pl.pallas_call(
        _ffn_kernel,
        grid=(M // tm,),
        in_specs=[
            pl.BlockSpec((tm, D), lambda i: (i, 0)),
            _const_spec((1, D)), _const_spec((1, D)),
            _const_spec((D, D_FF)), _const_spec((D_FF, D)),
        ],
        out_specs=pl.BlockSpec((tm, D), lambda i: (i, 0)),
        out_shape=jax.ShapeDtypeStruct((M, D), _F32),
        compiler_params=pltpu.CompilerParams(
            dimension_semantics=("arbitrary",), vmem_limit_bytes=V7X_VMEM_LIMIT_BYTES),
        name="ffn",
    )(x, row(gpre), row(gpost), wup_b, wdn_b)


def _gate_weights(w_a, w_x):
    per = GATE_CHUNK // RNN_BLOCK
    eye = jnp.eye(per, dtype=bool)[None, :, None, :, None]

    def bd(w):
        w = w.reshape(N_GATE_CHUNKS, per, RNN_BLOCK, 1, RNN_BLOCK)
        return jnp.where(eye, w, 0.0).reshape(N_GATE_CHUNKS, GATE_CHUNK, GATE_CHUNK)

    return jnp.concatenate([bd(w_a), bd(w_x)], axis=-1).astype(_BF16)


def kernel(x_prompt, x_sample, cache_k_win, cache_v_win, state_conv, state_lru, w_in, w_out, sinks, conv_w, conv_b,
           lru_w_a, lru_b_a, lru_w_x, lru_b_x, lru_lambda, w_up, w_down, g_pre_mix, g_post_mix, g_pre_ffn,
           g_post_ffn):
    B, T, D = x_prompt.shape
    DB, DT, _ = x_sample.shape
    depth = w_in.shape[0]
    assert depth == 1 and DT == 1 and T % TQ == 0 and DB % SAMPLE_BC == 0 and (B * T) % FFN_TM == 0
    l = 0
    win_b = w_in[l].astype(_BF16)
    wout_b = w_out[l].astype(_BF16)
    wup_b = w_up[l].astype(_BF16)
    wdn_b = w_down[l].astype(_BF16)
    wab = _gate_weights(lru_w_a[l], lru_w_x[l])
    layer = (g_pre_mix[l], g_post_mix[l], win_b, wout_b, conv_w[l], conv_b[l], wab, lru_b_a[l], lru_b_x[l],
             lru_lambda[l])

    x1p, kp, vp, cp, hp = _mix_prompt(x_prompt, sinks[l], *layer)
    yp = _ffn(x1p.reshape(B * T, D), g_pre_ffn[l], g_post_ffn[l], wup_b, wdn_b, FFN_TM).reshape(B, T, D)

    x1s, kq, vq, cq, hq = _mix_sample(
        x_sample.reshape(DB, D), cache_k_win[l].reshape(DB, WINDOW, D_KV), cache_v_win[l].reshape(DB, WINDOW, D_KV),
        jnp.swapaxes(state_conv[l], 0, 1), state_lru[l], sinks[l], *layer)
    ys = _ffn(x1s, g_pre_ffn[l], g_post_ffn[l], wup_b, wdn_b, DB).reshape(DB, DT, D)

    kv_shape = (1, -1, WINDOW, N_KV_HEADS, HEAD_DIM)
    return (yp, ys, kp.reshape(kv_shape), vp.reshape(kv_shape), cp[None], hp.reshape(1, B, D_RNN),
            kq.reshape(kv_shape), vq.reshape(kv_shape), jnp.swapaxes(cq, 0, 1)[None], hq[None])
```

```python
import functools

import jax
import jax.numpy as jnp
from jax import lax
from jax.experimental import pallas as pl
from jax.experimental.pallas import tpu as pltpu

D_MODEL = 1024
N_HEADS = 16
N_KV_HEADS = 4
HEAD_DIM = 64
GROUP = N_HEADS // N_KV_HEADS
D_ATTN = N_HEADS * HEAD_DIM
D_KV = N_KV_HEADS * HEAD_DIM
WINDOW = 128
D_RNN = D_MODEL
N_RNN_BLOCKS = 16
RNN_BLOCK = D_RNN // N_RNN_BLOCKS
CONV_WIDTH = 4
LRU_C = 8.0
D_FF = 4 * D_MODEL
EPS = 1e-6

OFF_Q = 0
OFF_K = D_ATTN
OFF_V = D_ATTN + D_KV
OFF_U = D_ATTN + 2 * D_KV
OFF_G = OFF_U + D_RNN
OFF_GA = OFF_G + D_RNN
OFF_GR = OFF_GA + D_MODEL
D_IN = OFF_GR + D_MODEL

V7X_SUBLANES = 8
V7X_LANES = 128
V7X_MXU_DIM = 256
V7X_VMEM_LIMIT_BYTES = 56 * 1024 * 1024

TQ = WINDOW
GATE_CHUNK = V7X_MXU_DIM
N_GATE_CHUNKS = D_RNN // GATE_CHUNK
SAMPLE_BC = 8
FFN_UP_COLS = 512

_BF16 = jnp.bfloat16
_F32 = jnp.float32


def _dot(a, b):
    return jnp.dot(a, b, preferred_element_type=_F32)


def _dot_nt(a, b):
    return lax.dot_general(a, b, (((1,), (1,)), ((), ())), preferred_element_type=_F32)


def _dot_col_tiles(a, w_ref):
    return jnp.concatenate([_dot(a, w_ref[j]) for j in range(w_ref.shape[0])], axis=1)


def _col_tiles(w, cols):
    k, n = w.shape
    return jnp.transpose(w.astype(_BF16).reshape(k, n // cols, cols), (1, 0, 2))


def _rmsnorm(x, g):
    return x * lax.rsqrt(jnp.mean(x * x, axis=-1, keepdims=True) + EPS) * g


def _softplus(y):
    return jnp.maximum(y, 0.0) + jnp.log1p(jnp.exp(-jnp.abs(y)))


def _lane_block(shape):
    return lax.broadcasted_iota(jnp.int32, shape, len(shape) - 1) // HEAD_DIM


def _lru_gates(uc, wab, ba, bx, lam):
    gc = _dot(uc.astype(_BF16), wab)
    r = jax.nn.sigmoid(gc[:, :GATE_CHUNK] + ba)
    i = jax.nn.sigmoid(gc[:, GATE_CHUNK:] + bx)
    log_a = -LRU_C * r * _softplus(-lam)
    a = jnp.exp(log_a)
    m2 = -jnp.tanh(log_a) * (a * a + 1.0)
    mult = jnp.where(m2 > 0.0, m2 * lax.rsqrt(m2), 0.0)
    return a, mult, i


def _replicate_head_blocks(m):
    lo, hi = m[:, :V7X_LANES], m[:, V7X_LANES:]
    rlo = pltpu.roll(lo, HEAD_DIM, axis=1)
    rhi = pltpu.roll(hi, HEAD_DIM, axis=1)
    first = lax.broadcasted_iota(jnp.int32, lo.shape, 1) < HEAD_DIM
    halves = (jnp.where(first, lo, rlo), jnp.where(first, rlo, lo),
              jnp.where(first, hi, rhi), jnp.where(first, rhi, hi))
    return [jnp.concatenate([b, b], axis=1) for b in halves]


N_SCRATCH = 15


def _layer_prompt_kernel(*refs, nt, n_blocks):
    s = pl.program_id(0)
    scratch = refs[-N_SCRATCH:]

    @pl.when(s == 0)
    def _():
        for ref in scratch:
            ref[...] = jnp.zeros(ref.shape, ref.dtype)

    for par in (0, 1):
        pl.when(s % 2 == par)(functools.partial(_layer_prompt_step, par, s, *refs, nt=nt, n_blocks=n_blocks))


def _layer_prompt_step(par, s, sinks_ref, xa_ref, xb_ref, gpre_ref, gpost_ref, win_ref, wout_ref, convw_ref,
                       convb_ref, wab_ref, ba_ref, bx_ref, lam_ref, gpre2_ref, gpost2_ref, wup_ref, wdn_ref,
                       y_ref, kwin_ref, vwin_ref, convst_ref, lru_ref,
                       z0_ref, z1_ref, x10_ref, x11_ref, hn0_ref, hn1_ref, kprev_ref, vprev_ref, uext_ref, hcarry_ref,
                       act_ref, f_ref, a_ref, b_ref, h_ref, *, nt, n_blocks):
    za_ref, zb_ref = (z0_ref, z1_ref) if par == 0 else (z1_ref, z0_ref)
    x1c_ref, x1b_ref = (x10_ref, x11_ref) if par == 0 else (x11_ref, x10_ref)
    hnc_ref, hnb_ref = (hn0_ref, hn1_ref) if par == 0 else (hn1_ref, hn0_ref)

    xnb = _rmsnorm(xa_ref[0], gpre_ref[...]).astype(_BF16)

    def a_tile(j):
        c = j * V7X_MXU_DIM
        za_ref[:, c:c + V7X_MXU_DIM] = _dot(xnb, win_ref[:, c:c + V7X_MXU_DIM])

    def c_up(j):
        c = j * FFN_UP_COLS
        up = _dot(hnc_ref[...], wup_ref[j])
        act_ref[:, c:c + FFN_UP_COLS] = jnp.square(jnp.maximum(up, 0.0)).astype(_BF16)

    def c_down(j):
        c = j * V7X_MXU_DIM
        f_ref[:, c:c + V7X_MXU_DIM] = _dot(act_ref[...], wdn_ref[j])

    def c_tail():
        y_ref[0] = x1c_ref[...] + _rmsnorm(f_ref[...], gpost2_ref[...])

    n_a, n_up, n_dn = D_IN // V7X_MXU_DIM, D_FF // FFN_UP_COLS, D_MODEL // V7X_MXU_DIM
    w_up, w_dn = FFN_UP_COLS // V7X_MXU_DIM, D_FF // D_MODEL
    a_iter = iter([(1, functools.partial(a_tile, j)) for j in range(n_a)])
    pieces = []
    for j in range(n_up):
        pieces.append((w_up, functools.partial(c_up, j)))
        pieces.extend(p for _, p in zip(range(2), a_iter))
    for j in range(n_dn):
        pieces.append((w_dn, functools.partial(c_down, j)))
        pieces.extend(p for _, p in zip(range(1), a_iter))
    pieces.append((0, c_tail))
    pieces.extend(a_iter)
    pieces.reverse()
    credit = [0.0]

    def fill(units):
        credit[0] += units
        while pieces and credit[0] >= 0.5 * pieces[-1][0]:
            weight, emit = pieces.pop()
            emit()
            credit[0] -= weight

    fill(2)

    t = jnp.maximum(s - 1, 0) % nt
    assert nt % 2 == 0
    rd, wr = 1 - par, par
    base = V7X_SUBLANES

    k = zb_ref[:, OFF_K:OFF_V]
    v = zb_ref[:, OFF_V:OFF_U]
    kwin_ref[0] = k
    vwin_ref[0] = v
    kcur = [a.astype(_BF16) for a in _replicate_head_blocks(k)]
    vcur = [a.astype(_BF16) for a in _replicate_head_blocks(v)]
    for h in range(N_KV_HEADS):
        kprev_ref[wr * N_KV_HEADS + h] = kcur[h]
        vprev_ref[wr * N_KV_HEADS + h] = vcur[h]

    qi = lax.broadcasted_iota(jnp.int32, (TQ, 2 * TQ), 0)
    kj = lax.broadcasted_iota(jnp.int32, (TQ, 2 * TQ), 1)
    diff = qi + TQ - kj
    first_key = jnp.where(t > 0, 0, TQ)
    mask = (diff >= 0) & (diff <= WINDOW) & (kj >= first_key)
    blk = _lane_block((TQ, D_KV))
    scale = HEAD_DIM ** -0.5

    def attention_head(h):
        qc = zb_ref[:, h * D_KV:(h + 1) * D_KV] * scale
        qm = jnp.concatenate([jnp.where(blk == g, qc, 0.0).astype(_BF16) for g in range(GROUP)], axis=0)
        kk = jnp.concatenate([kprev_ref[rd * N_KV_HEADS + h], kcur[h]], axis=0)
        vv = jnp.concatenate([vprev_ref[rd * N_KV_HEADS + h], vcur[h]], axis=0)
        sc = _dot_nt(qm, kk)
        ps, invs = [], []
        for g in range(GROUP):
            sg = jnp.where(mask, sc[g * TQ:(g + 1) * TQ], -jnp.inf)
            sink = sinks_ref[h * GROUP + g]
            m = jnp.maximum(jnp.max(sg, axis=-1, keepdims=True), sink)
            p = jnp.exp(sg - m)
            den = jnp.sum(p, axis=-1, keepdims=True) + jnp.exp(sink - m)
            invs.append(1.0 / den)
            ps.append(p.astype(_BF16))
            if g % 2 == 1:
                fill(1)
        o = _dot(jnp.concatenate(ps, axis=0), vv)
        acc = jnp.where(blk == 0, o[0:TQ] * invs[0], 0.0)
        for g in range(1, GROUP):
            acc = acc + jnp.where(blk == g, o[g * TQ:(g + 1) * TQ] * invs[g], 0.0)
        fill(1)
        return acc

    def scan_groups(carry, first, last):
        sub = lax.broadcasted_iota(jnp.int32, (V7X_SUBLANES, D_RNN), 0)
        for j in range(first, last):
            rows = slice(j * V7X_SUBLANES, (j + 1) * V7X_SUBLANES)
            aj = a_ref[rows, :]
            bj = b_ref[rows, :]
            for sft in (1, 2, 4):
                keep = sub >= sft
                bj = jnp.where(keep, aj * pltpu.roll(bj, sft, axis=0) + bj, bj)
                aj = jnp.where(keep, aj * pltpu.roll(aj, sft, axis=0), aj)
            hj = aj * carry + bj
            h_ref[rows, :] = hj
            if j % 2 == 1:
                fill(2 if j % 4 == 1 else 1)
            carry = jnp.broadcast_to(hj[V7X_SUBLANES - 1:V7X_SUBLANES, :], (V7X_SUBLANES, D_RNN))
        return carry

    pos0 = (t * TQ + lax.broadcasted_iota(jnp.int32, (TQ, GATE_CHUNK), 0)) == 0
    for c in range(N_GATE_CHUNKS):
        cols = slice(c * GATE_CHUNK, (c + 1) * GATE_CHUNK)
        u = zb_ref[:, OFF_U + c * GATE_CHUNK:OFF_U + (c + 1) * GATE_CHUNK]
        uext_ref[rd, 0:base, cols] = jnp.where(t > 0, uext_ref[rd, 0:base, cols], 0.0)
        uext_ref[rd, base:base + TQ, cols] = u
        cw = convw_ref[:, cols]
        cs = uext_ref[rd, base - 3:base - 3 + TQ, cols] * cw[0:1]
        cs = cs + uext_ref[rd, base - 2:base - 2 + TQ, cols] * cw[1:2]
        cs = cs + uext_ref[rd, base - 1:base - 1 + TQ, cols] * cw[2:3]
        cs = cs + u * cw[3:4]
        uc = convb_ref[:, cols] + cs
        convst_ref[0, :, cols] = uext_ref[rd, base + TQ - (CONV_WIDTH - 1):base + TQ, cols]
        uext_ref[wr, 0:base, cols] = u[TQ - base:TQ, :]
        fill(2)
        a, mult, i = _lru_gates(uc, wab_ref[c], ba_ref[:, cols], bx_ref[:, cols], lam_ref[:, cols])
        a_ref[:, cols] = a
        b_ref[:, cols] = jnp.where(pos0, 1.0, mult) * i * uc
        fill(2)

    attn_chunks = [attention_head(0), attention_head(1)]
    n_groups = TQ // V7X_SUBLANES
    carry = scan_groups(jnp.where(t > 0, hcarry_ref[rd], 0.0), 0, n_groups // 2)
    attn_chunks.append(attention_head(2))
    carry = scan_groups(carry, n_groups // 2, n_groups)
    hcarry_ref[wr] = carry
    lru_ref[0] = jnp.where(s <= n_blocks, carry[0:1, :], lru_ref[0])
    attn_chunks.append(attention_head(3))


    merged = []
    for c in range(D_MODEL // V7X_MXU_DIM):
        cols = slice(c * V7X_MXU_DIM, (c + 1) * V7X_MXU_DIM)
        zcol = lambda off: zb_ref[:, off + c * V7X_MXU_DIM:off + (c + 1) * V7X_MXU_DIM]
        rnn_out = jax.nn.gelu(zcol(OFF_G)) * h_ref[:, cols]
        merged.append((jax.nn.sigmoid(zcol(OFF_GA)) * attn_chunks[c]
                       + jax.nn.sigmoid(zcol(OFF_GR)) * rnn_out).astype(_BF16))
        fill(1)
    mix = _dot_col_tiles(jnp.concatenate(merged, axis=1), wout_ref)
    fill(sum(w for w, _ in pieces) + len(pieces))
    assert not pieces
    x1 = xb_ref[0] + _rmsnorm(mix, gpost_ref[...])
    x1b_ref[...] = x1
    hnb_ref[...] = _rmsnorm(x1, gpre2_ref[...]).astype(_BF16)


def _const_spec(shape):
    zeros = (0,) * len(shape)
    return pl.BlockSpec(shape, lambda *_: zeros, pipeline_mode=pl.Buffered(1))


def _layer_prompt(x, sinks, gpre, gpost, win_b, wout_b, convw, convb, wab, ba, bx, lam, gpre2, gpost2, wup_b, wdn_b):
    B, T, D = x.shape
    nt = T // TQ
    n_blocks = B * nt
    row = lambda a: a.reshape(1, -1)
    cur = lambda s: jnp.minimum(s, n_blocks - 1)
    prv = lambda s: jnp.clip(s - 1, 0, n_blocks - 1)
    ppv = lambda s: jnp.maximum(s - 2, 0)
    return pl.pallas_call(
        functools.partial(_layer_prompt_kernel, nt=nt, n_blocks=n_blocks),
        grid=(n_blocks + 2,),
        in_specs=[
            pl.BlockSpec(memory_space=pltpu.SMEM),
            pl.BlockSpec((1, TQ, D), lambda s: (cur(s) // nt, cur(s) % nt, 0)),
            pl.BlockSpec((1, TQ, D), lambda s: (prv(s) // nt, prv(s) % nt, 0)),
            _const_spec((1, D)), _const_spec((1, D)),
            _const_spec((D, D_IN)), _const_spec((D // V7X_MXU_DIM, D, V7X_MXU_DIM)),
            _const_spec((CONV_WIDTH, D_RNN)), _const_spec((1, D_RNN)),
            _const_spec((N_GATE_CHUNKS, GATE_CHUNK, 2 * GATE_CHUNK)),
            _const_spec((1, D_RNN)), _const_spec((1, D_RNN)), _const_spec((1, D_RNN)),
            _const_spec((1, D)), _const_spec((1, D)),
            _const_spec((D_FF // FFN_UP_COLS, D, FFN_UP_COLS)), _const_spec((D // V7X_MXU_DIM, D_FF, V7X_MXU_DIM)),
        ],
        out_specs=[
            pl.BlockSpec((1, TQ, D), lambda s: (ppv(s) // nt, ppv(s) % nt, 0)),
            pl.BlockSpec((1, WINDOW, D_KV), lambda s: (prv(s) // nt, 0, 0)),
            pl.BlockSpec((1, WINDOW, D_KV), lambda s: (prv(s) // nt, 0, 0)),
            pl.BlockSpec((1, CONV_WIDTH - 1, D_RNN), lambda s: (prv(s) // nt, 0, 0)),
            pl.BlockSpec((1, 1, D_RNN), lambda s: (prv(s) // nt, 0, 0)),
        ],
        out_shape=[
            jax.ShapeDtypeStruct((B, T, D), _F32),
            jax.ShapeDtypeStruct((B, WINDOW, D_KV), _F32),
            jax.ShapeDtypeStruct((B, WINDOW, D_KV), _F32),
            jax.ShapeDtypeStruct((B, CONV_WIDTH - 1, D_RNN), _F32),
            jax.ShapeDtypeStruct((B, 1, D_RNN), _F32),
        ],
        scratch_shapes=[
            pltpu.VMEM((TQ, D_IN), _F32), pltpu.VMEM((TQ, D_IN), _F32),
            pltpu.VMEM((TQ, D), _F32), pltpu.VMEM((TQ, D), _F32),
            pltpu.VMEM((TQ, D), _BF16), pltpu.VMEM((TQ, D), _BF16),
            pltpu.VMEM((2 * N_KV_HEADS, TQ, D_KV), _BF16),
            pltpu.VMEM((2 * N_KV_HEADS, TQ, D_KV), _BF16),
            pltpu.VMEM((2, V7X_SUBLANES + TQ, D_RNN), _F32),
            pltpu.VMEM((2, V7X_SUBLANES, D_RNN), _F32),
            pltpu.VMEM((TQ, D_FF), _BF16), pltpu.VMEM((TQ, D), _F32),
            pltpu.VMEM((TQ, D_RNN), _F32), pltpu.VMEM((TQ, D_RNN), _F32), pltpu.VMEM((TQ, D_RNN), _F32),
        ],
        compiler_params=pltpu.CompilerParams(
            dimension_semantics=("arbitrary",), vmem_limit_bytes=V7X_VMEM_LIMIT_BYTES),
        name="layer_prompt",
    )(sinks, x, x, row(gpre), row(gpost), win_b, wout_b, convw, row(convb), wab, row(ba), row(bx), row(lam),
      row(gpre2), row(gpost2), wup_b, wdn_b)


def _mix_sample_kernel(x_ref, ck_ref, cv_ref, cst_ref, h0_ref, sinks_ref, gpre_ref, gpost_ref, win_ref, wout_ref,
                       convw_ref, convb_ref, wab_ref, ba_ref, bx_ref, lam_ref,
                       x1_ref, kwin_ref, vwin_ref, convst_ref, lru_ref,
                       qrow_ref, orow_ref, knew_ref, vnew_ref, rnn_ref, siga_ref):
    step = pl.program_id(0)
    last = pl.num_programs(0) - 1
    nb = x_ref.shape[0]
    blk = _lane_block((nb, D_KV))

    @pl.when(step == 0)
    def _():
        xnb = _rmsnorm(x_ref[...], gpre_ref[...]).astype(_BF16)
        qkv = _dot(xnb, win_ref[:, OFF_Q:OFF_U])
        knew_ref[...] = qkv[:, OFF_K:OFF_V]
        vnew_ref[...] = qkv[:, OFF_V:OFF_U]
        scale = HEAD_DIM ** -0.5
        for h in range(N_KV_HEADS):
            qc = qkv[:, h * D_KV:(h + 1) * D_KV] * scale
            for g in range(GROUP):
                shift = ((h - g) % GROUP) * HEAD_DIM
                moved = pltpu.roll(qc, shift, axis=1) if shift else qc
                hd = h * GROUP + g
                placed = jnp.where(blk == h, moved, 0.0)
                for half in range(D_KV // V7X_LANES):
                    qrow_ref[half, hd * nb:(hd + 1) * nb, :] = placed[:, half * V7X_LANES:(half + 1) * V7X_LANES]

        u = _dot(xnb, win_ref[:, OFF_U:OFF_G])
        cw = convw_ref[...]
        cs = cst_ref[0] * cw[0:1]
        cs = cs + cst_ref[1] * cw[1:2]
        cs = cs + cst_ref[2] * cw[2:3]
        cs = cs + u * cw[3:4]
        uc = convb_ref[...] + cs
        convst_ref[0] = cst_ref[1]
        convst_ref[1] = cst_ref[2]
        convst_ref[2] = u
        parts = [_lru_gates(uc[:, c * GATE_CHUNK:(c + 1) * GATE_CHUNK], wab_ref[c],
                            *(r[:, c * GATE_CHUNK:(c + 1) * GATE_CHUNK] for r in (ba_ref, bx_ref, lam_ref)))
                 for c in range(N_GATE_CHUNKS)]
        a, mult, i = (jnp.concatenate(p, axis=1) for p in zip(*parts))
        hnew = a * h0_ref[...] + mult * i * uc
        lru_ref[...] = hnew
        g_branch = _dot(xnb, win_ref[:, OFF_G:OFF_GA])
        gate_r = _dot(xnb, win_ref[:, OFF_GR:D_IN])
        rnn_ref[...] = jax.nn.sigmoid(gate_r) * (jax.nn.gelu(g_branch) * hnew)
        siga_ref[...] = jax.nn.sigmoid(_dot(xnb, win_ref[:, OFF_GA:OFF_GR]))

    sink = sinks_ref[...]
    rowid = lax.broadcasted_iota(jnp.int32, (WINDOW, D_KV), 0)

    def per_row(bl, carry):
        b = step * SAMPLE_BC + bl
        kb = ck_ref[bl]
        vb = cv_ref[bl]
        kn = knew_ref[pl.ds(b, 1), :]
        vn = vnew_ref[pl.ds(b, 1), :]
        kwin_ref[bl] = jnp.where(rowid == WINDOW - 1, kn, pltpu.roll(kb, WINDOW - 1, axis=0))
        vwin_ref[bl] = jnp.where(rowid == WINDOW - 1, vn, pltpu.roll(vb, WINDOW - 1, axis=0))

        head_rows = pl.ds(b, N_HEADS, stride=nb)
        qm = jnp.concatenate([qrow_ref[half, head_rows, :] for half in range(D_KV // V7X_LANES)],
                             axis=1).astype(_BF16)
        s = _dot_nt(qm, kb.astype(_BF16))
        knr = kn.astype(_BF16).astype(_F32)
        s_self = jnp.sum(qm.astype(_F32) * knr, axis=-1, keepdims=True)
        m = jnp.maximum(jnp.maximum(jnp.max(s, axis=-1, keepdims=True), s_self), sink)
        p = jnp.exp(s - m)
        p_self = jnp.exp(s_self - m)
        den = jnp.sum(p, axis=-1, keepdims=True) + p_self + jnp.exp(sink - m)
        o = _dot(p.astype(_BF16), vb.astype(_BF16))
        o = o + p_self.astype(_BF16).astype(_F32) * vn.astype(_BF16).astype(_F32)
        o = o * (1.0 / den)
        for half in range(D_KV // V7X_LANES):
            orow_ref[half, head_rows, :] = o[:, half * V7X_LANES:(half + 1) * V7X_LANES]
        return carry

    lax.fori_loop(0, SAMPLE_BC, per_row, 0)

    @pl.when(step == last)
    def _():
        chunks = []
        for h in range(N_KV_HEADS):
            acc = None
            for g in range(GROUP):
                hd = h * GROUP + g
                rows = jnp.concatenate([orow_ref[half, hd * nb:(hd + 1) * nb, :]
                                        for half in range(D_KV // V7X_LANES)], axis=1)
                sel = jnp.where(blk == h, rows, 0.0)
                shift = ((g - h) % GROUP) * HEAD_DIM
                moved = pltpu.roll(sel, shift, axis=1) if shift else sel
                acc = moved if acc is None else acc + moved
            chunks.append(acc)
        attn = jnp.concatenate(chunks, axis=1)
        merged = (siga_ref[...] * attn + rnn_ref[...]).astype(_BF16)
        mix = _dot_col_tiles(merged, wout_ref)
        x1_ref[...] = x_ref[...] + _rmsnorm(mix, gpost_ref[...])


def _mix_sample(x, ck, cv, cst, h0, sinks, gpre, gpost, win_b, wout_b, convw, convb, wab, ba, bx, lam):
    nb, D = x.shape
    row = lambda a: a.reshape(1, -1)
    cache_spec = pl.BlockSpec((SAMPLE_BC, WINDOW, D_KV), lambda s: (s, 0, 0))
    return pl.pallas_call(
        _mix_sample_kernel,
        grid=(nb // SAMPLE_BC,),
        in_specs=[
            _const_spec((nb, D)), cache_spec, cache_spec,
            _const_spec((CONV_WIDTH - 1, nb, D_RNN)), _const_spec((nb, D_RNN)),
            _const_spec((N_HEADS, 1)),
            _const_spec((1, D)), _const_spec((1, D)),
            _const_spec((D, D_IN)), _const_spec((D // V7X_MXU_DIM, D, V7X_MXU_DIM)),
            _const_spec((CONV_WIDTH, D_RNN)), _const_spec((1, D_RNN)),
            _const_spec((N_GATE_CHUNKS, GATE_CHUNK, 2 * GATE_CHUNK)),
            _const_spec((1, D_RNN)), _const_spec((1, D_RNN)), _const_spec((1, D_RNN)),
        ],
        out_specs=[
            pl.BlockSpec((nb, D), lambda s: (0, 0)),
            cache_spec, cache_spec,
            pl.BlockSpec((CONV_WIDTH - 1, nb, D_RNN), lambda s: (0, 0, 0)),
            pl.BlockSpec((nb, D_RNN), lambda s: (0, 0)),
        ],
        out_shape=[
            jax.ShapeDtypeStruct((nb, D), _F32),
            jax.ShapeDtypeStruct((nb, WINDOW, D_KV), _F32),
            jax.ShapeDtypeStruct((nb, WINDOW, D_KV), _F32),
            jax.ShapeDtypeStruct((CONV_WIDTH - 1, nb, D_RNN), _F32),
            jax.ShapeDtypeStruct((nb, D_RNN), _F32),
        ],
        scratch_shapes=[
            pltpu.VMEM((D_KV // V7X_LANES, N_HEADS * nb, V7X_LANES), _F32),
            pltpu.VMEM((D_KV // V7X_LANES, N_HEADS * nb, V7X_LANES), _F32),
            pltpu.VMEM((nb, D_KV), _F32), pltpu.VMEM((nb, D_KV), _F32),
            pltpu.VMEM((nb, D_RNN), _F32), pltpu.VMEM((nb, D), _F32),
        ],
        compiler_params=pltpu.CompilerParams(
            dimension_semantics=("arbitrary",), vmem_limit_bytes=V7X_VMEM_LIMIT_BYTES),
        name="mix_sample",
    )(x, ck, cv, cst, h0, sinks.reshape(N_HEADS, 1), row(gpre), row(gpost), win_b, wout_b, convw, row(convb),
      wab, row(ba), row(bx), row(lam))


def _ffn_kernel(x_ref, gpre_ref, gpost_ref, wup_ref, wdn_ref, o_ref):
    x = x_ref[...]
    hn = _rmsnorm(x, gpre_ref[...]).astype(_BF16)
    up = _dot_col_tiles(hn, wup_ref)
    act = jnp.square(jnp.maximum(up, 0.0)).astype(_BF16)
    f = _dot_col_tiles(act, wdn_ref)
    o_ref[...] = x + _rmsnorm(f, gpost_ref[...])


def _ffn(x, gpre, gpost, wup_b, wdn_b, tm):
    M, D = x.shape
    row = lambda a: a.reshape(1, -1)
    return pl.pallas_call(
        _ffn_kernel,
        grid=(M // tm,),
        in_specs=[
            pl.BlockSpec((tm, D), lambda i: (i, 0)),
            _const_spec((1, D)), _const_spec((1, D)),
            _const_spec((D_FF // FFN_UP_COLS, D, FFN_UP_COLS)), _const_spec((D // V7X_MXU_DIM, D_FF, V7X_MXU_DIM)),
        ],
        out_specs=pl.BlockSpec((tm, D), lambda i: (i, 0)),
        out_shape=jax.ShapeDtypeStruct((M, D), _F32),
        compiler_params=pltpu.CompilerParams(
            dimension_semantics=("arbitrary",), vmem_limit_bytes=V7X_VMEM_LIMIT_BYTES),
        name="ffn",
    )(x, row(gpre), row(gpost), wup_b, wdn_b)


def _gate_weights(w_a, w_x):
    per = GATE_CHUNK // RNN_BLOCK
    eye = jnp.eye(per, dtype=bool)[None, :, None, :, None]

    def bd(w):
        w = w.reshape(N_GATE_CHUNKS, per, RNN_BLOCK, 1, RNN_BLOCK)
        return jnp.where(eye, w, 0.0).reshape(N_GATE_CHUNKS, GATE_CHUNK, GATE_CHUNK)

    return jnp.concatenate([bd(w_a), bd(w_x)], axis=-1).astype(_BF16)


def kernel(x_prompt, x_sample, cache_k_win, cache_v_win, state_conv, state_lru, w_in, w_out, sinks, conv_w, conv_b,
           lru_w_a, lru_b_a, lru_w_x, lru_b_x, lru_lambda, w_up, w_down, g_pre_mix, g_post_mix, g_pre_ffn,
           g_post_ffn):
    B, T, D = x_prompt.shape
    DB, DT, _ = x_sample.shape
    depth = w_in.shape[0]
    assert depth == 1 and DT == 1 and T % (2 * TQ) == 0 and DB % SAMPLE_BC == 0
    l = 0
    win_b = w_in[l].astype(_BF16)
    wout_b = _col_tiles(w_out[l], V7X_MXU_DIM)
    wup_b = _col_tiles(w_up[l], FFN_UP_COLS)
    wdn_b = _col_tiles(w_down[l], V7X_MXU_DIM)
    wab = _gate_weights(lru_w_a[l], lru_w_x[l])
    layer = (g_pre_mix[l], g_post_mix[l], win_b, wout_b, conv_w[l], conv_b[l], wab, lru_b_a[l], lru_b_x[l],
             lru_lambda[l])

    yp, kp, vp, cp, hp = _layer_prompt(x_prompt, sinks[l], *layer, g_pre_ffn[l], g_post_ffn[l], wup_b, wdn_b)

    x1s, kq, vq, cq, hq = _mix_sample(
        x_sample.reshape(DB, D), cache_k_win[l].reshape(DB, WINDOW, D_KV), cache_v_win[l].reshape(DB, WINDOW, D_KV),
        jnp.swapaxes(state_conv[l], 0, 1), state_lru[l], sinks[l], *layer)
    ys = _ffn(x1s, g_pre_ffn[l], g_post_ffn[l], wup_b, wdn_b, DB).reshape(DB, DT, D)

    kv_shape = (1, -1, WINDOW, N_KV_HEADS, HEAD_DIM)
    return (yp, ys, kp.reshape(kv_shape), vp.reshape(kv_shape), cp[None], hp.reshape(1, B, D_RNN),
            kq.reshape(kv_shape), vq.reshape(kv_shape), jnp.swapaxes(cq, 0, 1)[None], hq[None])
```

```python
import functools

import jax
import jax.numpy as jnp
from jax import lax
from jax.experimental import pallas as pl
from jax.experimental.pallas import tpu as pltpu

D_MODEL = 1024
N_HEADS = 16
N_KV_HEADS = 4
HEAD_DIM = 64
GROUP = N_HEADS // N_KV_HEADS
D_ATTN = N_HEADS * HEAD_DIM
D_KV = N_KV_HEADS * HEAD_DIM
WINDOW = 128
D_RNN = D_MODEL
N_RNN_BLOCKS = 16
RNN_BLOCK = D_RNN // N_RNN_BLOCKS
CONV_WIDTH = 4
LRU_C = 8.0
D_FF = 4 * D_MODEL
EPS = 1e-6

OFF_Q = 0
OFF_K = D_ATTN
OFF_V = D_ATTN + D_KV
OFF_U = D_ATTN + 2 * D_KV
OFF_G = OFF_U + D_RNN
OFF_GA = OFF_G + D_RNN
OFF_GR = OFF_GA + D_MODEL
D_IN = OFF_GR + D_MODEL

V7X_SUBLANES = 8
V7X_LANES = 128
V7X_MXU_DIM = 256
V7X_VMEM_LIMIT_BYTES = 56 * 1024 * 1024

TQ = WINDOW
GATE_CHUNK = V7X_MXU_DIM
N_GATE_CHUNKS = D_RNN // GATE_CHUNK
SAMPLE_BC = 8
FFN_UP_COLS = 512

_BF16 = jnp.bfloat16
_F32 = jnp.float32


def _dot(a, b):
    return jnp.dot(a, b, preferred_element_type=_F32)


def _dot_nt(a, b):
    return lax.dot_general(a, b, (((1,), (1,)), ((), ())), preferred_element_type=_F32)


def _dot_col_tiles(a, w_ref):
    return jnp.concatenate([_dot(a, w_ref[j]) for j in range(w_ref.shape[0])], axis=1)


def _col_tiles(w, cols):
    k, n = w.shape
    return jnp.transpose(w.astype(_BF16).reshape(k, n // cols, cols), (1, 0, 2))


def _rmsnorm(x, g):
    return x * lax.rsqrt(jnp.mean(x * x, axis=-1, keepdims=True) + EPS) * g


def _softplus(y):
    return jnp.maximum(y, 0.0) + jnp.log1p(jnp.exp(-jnp.abs(y)))


def _lane_block(shape):
    return lax.broadcasted_iota(jnp.int32, shape, len(shape) - 1) // HEAD_DIM


def _lru_gates(uc, wab, ba, bx, lam):
    gc = _dot(uc.astype(_BF16), wab)
    r = jax.nn.sigmoid(gc[:, :GATE_CHUNK] + ba)
    i = jax.nn.sigmoid(gc[:, GATE_CHUNK:] + bx)
    log_a = -LRU_C * r * _softplus(-lam)
    a = jnp.exp(log_a)
    m2 = -jnp.tanh(log_a) * (a * a + 1.0)
    mult = jnp.where(m2 > 0.0, m2 * lax.rsqrt(m2), 0.0)
    return a, mult, i


def _replicate_head_blocks(m):
    lo, hi = m[:, :V7X_LANES], m[:, V7X_LANES:]
    rlo = pltpu.roll(lo, HEAD_DIM, axis=1)
    rhi = pltpu.roll(hi, HEAD_DIM, axis=1)
    first = lax.broadcasted_iota(jnp.int32, lo.shape, 1) < HEAD_DIM
    halves = (jnp.where(first, lo, rlo), jnp.where(first, rlo, lo),
              jnp.where(first, hi, rhi), jnp.where(first, rhi, hi))
    return [jnp.concatenate([b, b], axis=1) for b in halves]


N_SCRATCH = 12


def _layer_prompt_kernel(*refs, nt, n_blocks):
    s = pl.program_id(0)
    scratch = refs[-N_SCRATCH:]

    @pl.when(s == 0)
    def _():
        for ref in scratch:
            ref[...] = jnp.zeros(ref.shape, ref.dtype)

    _layer_prompt_step(s % 2, s, *refs, nt=nt, n_blocks=n_blocks)


def _layer_prompt_step(par, s, sinks_ref, xa_ref, xb_ref, gpre_ref, gpost_ref, win_ref, wout_ref, convw_ref,
                       convb_ref, wab_ref, ba_ref, bx_ref, lam_ref, gpre2_ref, gpost2_ref, wup_ref, wdn_ref,
                       y_ref, kwin_ref, vwin_ref, convst_ref, lru_ref,
                       z_ref, x1_ref, hn_ref, kprev_ref, vprev_ref, uext_ref, hcarry_ref,
                       act_ref, f_ref, a_ref, b_ref, h_ref, *, nt, n_blocks):
    za_ref, zb_ref = z_ref.at[par], z_ref.at[1 - par]
    x1c_ref, x1b_ref = x1_ref.at[par], x1_ref.at[1 - par]
    hnc_ref, hnb_ref = hn_ref.at[par], hn_ref.at[1 - par]

    xnb = _rmsnorm(xa_ref[0], gpre_ref[...]).astype(_BF16)

    def a_tile(j):
        c = j * V7X_MXU_DIM
        za_ref[:, c:c + V7X_MXU_DIM] = _dot(xnb, win_ref[:, c:c + V7X_MXU_DIM])

    def c_up(j):
        c = j * FFN_UP_COLS
        up = _dot(hnc_ref[...], wup_ref[j])
        act_ref[:, c:c + FFN_UP_COLS] = jnp.square(jnp.maximum(up, 0.0)).astype(_BF16)

    def c_down(j):
        c = j * V7X_MXU_DIM
        f_ref[:, c:c + V7X_MXU_DIM] = _dot(act_ref[...], wdn_ref[j])

    def c_tail():
        y_ref[0] = x1c_ref[...] + _rmsnorm(f_ref[...], gpost2_ref[...])

    n_a, n_up, n_dn = D_IN // V7X_MXU_DIM, D_FF // FFN_UP_COLS, D_MODEL // V7X_MXU_DIM
    w_up, w_dn = FFN_UP_COLS // V7X_MXU_DIM, D_FF // D_MODEL
    a_iter = iter([(1, functools.partial(a_tile, j)) for j in range(n_a)])
    pieces = []
    for j in range(n_up):
        pieces.append((w_up, functools.partial(c_up, j)))
        pieces.extend(p for _, p in zip(range(2), a_iter))
    for j in range(n_dn):
        pieces.append((w_dn, functools.partial(c_down, j)))
        pieces.extend(p for _, p in zip(range(1), a_iter))
    pieces.append((0, c_tail))
    pieces.extend(a_iter)
    pieces.reverse()
    credit = [0.0]

    def fill(units):
        credit[0] += units
        while pieces and credit[0] >= 0.5 * pieces[-1][0]:
            weight, emit = pieces.pop()
            emit()
            credit[0] -= weight

    fill(2)

    t = jnp.maximum(s - 1, 0) % nt
    assert nt % 2 == 0
    rd, wr = 1 - par, par
    base = V7X_SUBLANES

    k = zb_ref[:, OFF_K:OFF_V]
    v = zb_ref[:, OFF_V:OFF_U]
    kwin_ref[0] = k
    vwin_ref[0] = v
    kcur = [a.astype(_BF16) for a in _replicate_head_blocks(k)]
    vcur = [a.astype(_BF16) for a in _replicate_head_blocks(v)]
    for h in range(N_KV_HEADS):
        kprev_ref[wr * N_KV_HEADS + h] = kcur[h]
        vprev_ref[wr * N_KV_HEADS + h] = vcur[h]

    qi = lax.broadcasted_iota(jnp.int32, (TQ, 2 * TQ), 0)
    kj = lax.broadcasted_iota(jnp.int32, (TQ, 2 * TQ), 1)
    diff = qi + TQ - kj
    first_key = jnp.where(t > 0, 0, TQ)
    mask = (diff >= 0) & (diff <= WINDOW) & (kj >= first_key)
    blk = _lane_block((TQ, D_KV))
    scale = HEAD_DIM ** -0.5

    def attention_head(h):
        qc = zb_ref[:, h * D_KV:(h + 1) * D_KV] * scale
        qm = jnp.concatenate([jnp.where(blk == g, qc, 0.0).astype(_BF16) for g in range(GROUP)], axis=0)
        kk = jnp.concatenate([kprev_ref[rd * N_KV_HEADS + h], kcur[h]], axis=0)
        vv = jnp.concatenate([vprev_ref[rd * N_KV_HEADS + h], vcur[h]], axis=0)
        sc = _dot_nt(qm, kk)
        ps, invs = [], []
        for g in range(GROUP):
            sg = jnp.where(mask, sc[g * TQ:(g + 1) * TQ], -jnp.inf)
            sink = sinks_ref[h * GROUP + g]
            m = jnp.maximum(jnp.max(sg, axis=-1, keepdims=True), sink)
            p = jnp.exp(sg - m)
            den = jnp.sum(p, axis=-1, keepdims=True) + jnp.exp(sink - m)
            invs.append(1.0 / den)
            ps.append(p.astype(_BF16))
            if g % 2 == 1:
                fill(1)
        o = _dot(jnp.concatenate(ps, axis=0), vv)
        acc = jnp.where(blk == 0, o[0:TQ] * invs[0], 0.0)
        for g in range(1, GROUP):
            acc = acc + jnp.where(blk == g, o[g * TQ:(g + 1) * TQ] * invs[g], 0.0)
        fill(1)
        return acc

    def scan_groups(carry, first, last):
        sub = lax.broadcasted_iota(jnp.int32, (V7X_SUBLANES, D_RNN), 0)
        for j in range(first, last):
            rows = slice(j * V7X_SUBLANES, (j + 1) * V7X_SUBLANES)
            aj = a_ref[rows, :]
            bj = b_ref[rows, :]
            for sft in (1, 2, 4):
                keep = sub >= sft
                bj = jnp.where(keep, aj * pltpu.roll(bj, sft, axis=0) + bj, bj)
                aj = jnp.where(keep, aj * pltpu.roll(aj, sft, axis=0), aj)
            hj = aj * carry + bj
            h_ref[rows, :] = hj
            if j % 2 == 1:
                fill(2 if j % 4 == 1 else 1)
            carry = jnp.broadcast_to(hj[V7X_SUBLANES - 1:V7X_SUBLANES, :], (V7X_SUBLANES, D_RNN))
        return carry

    pos0 = (t * TQ + lax.broadcasted_iota(jnp.int32, (TQ, GATE_CHUNK), 0)) == 0
    for c in range(N_GATE_CHUNKS):
        cols = slice(c * GATE_CHUNK, (c + 1) * GATE_CHUNK)
        u = zb_ref[:, OFF_U + c * GATE_CHUNK:OFF_U + (c + 1) * GATE_CHUNK]
        uext_ref[rd, 0:base, cols] = jnp.where(t > 0, uext_ref[rd, 0:base, cols], 0.0)
        uext_ref[rd, base:base + TQ, cols] = u
        cw = convw_ref[:, cols]
        cs = uext_ref[rd, base - 3:base - 3 + TQ, cols] * cw[0:1]
        cs = cs + uext_ref[rd, base - 2:base - 2 + TQ, cols] * cw[1:2]
        cs = cs + uext_ref[rd, base - 1:base - 1 + TQ, cols] * cw[2:3]
        cs = cs + u * cw[3:4]
        uc = convb_ref[:, cols] + cs
        convst_ref[0, :, cols] = uext_ref[rd, base + TQ - (CONV_WIDTH - 1):base + TQ, cols]
        uext_ref[wr, 0:base, cols] = u[TQ - base:TQ, :]
        fill(2)
        a, mult, i = _lru_gates(uc, wab_ref[c], ba_ref[:, cols], bx_ref[:, cols], lam_ref[:, cols])
        a_ref[:, cols] = a
        b_ref[:, cols] = jnp.where(pos0, 1.0, mult) * i * uc
        fill(2)

    attn_chunks = [attention_head(0), attention_head(1)]
    n_groups = TQ // V7X_SUBLANES
    carry = scan_groups(jnp.where(t > 0, hcarry_ref[rd], 0.0), 0, n_groups // 2)
    attn_chunks.append(attention_head(2))
    carry = scan_groups(carry, n_groups // 2, n_groups)
    hcarry_ref[wr] = carry
    lru_ref[0] = jnp.where(s <= n_blocks, carry[0:1, :], lru_ref[0])
    attn_chunks.append(attention_head(3))


    merged = []
    for c in range(D_MODEL // V7X_MXU_DIM):
        cols = slice(c * V7X_MXU_DIM, (c + 1) * V7X_MXU_DIM)
        zcol = lambda off: zb_ref[:, off + c * V7X_MXU_DIM:off + (c + 1) * V7X_MXU_DIM]
        rnn_out = jax.nn.gelu(zcol(OFF_G)) * h_ref[:, cols]
        merged.append((jax.nn.sigmoid(zcol(OFF_GA)) * attn_chunks[c]
                       + jax.nn.sigmoid(zcol(OFF_GR)) * rnn_out).astype(_BF16))
        fill(1)
    mix = _dot_col_tiles(jnp.concatenate(merged, axis=1), wout_ref)
    fill(sum(w for w, _ in pieces) + len(pieces))
    assert not pieces
    x1 = xb_ref[0] + _rmsnorm(mix, gpost_ref[...])
    x1b_ref[...] = x1
    hnb_ref[...] = _rmsnorm(x1, gpre2_ref[...]).astype(_BF16)


def _const_spec(shape):
    zeros = (0,) * len(shape)
    return pl.BlockSpec(shape, lambda *_: zeros, pipeline_mode=pl.Buffered(1))


def _layer_prompt(x, sinks, gpre, gpost, win_b, wout_b, convw, convb, wab, ba, bx, lam, gpre2, gpost2, wup_b, wdn_b):
    B, T, D = x.shape
    nt = T // TQ
    n_blocks = B * nt
    row = lambda a: a.reshape(1, -1)
    cur = lambda s: jnp.minimum(s, n_blocks - 1)
    prv = lambda s: jnp.clip(s - 1, 0, n_blocks - 1)
    ppv = lambda s: jnp.maximum(s - 2, 0)
    return pl.pallas_call(
        functools.partial(_layer_prompt_kernel, nt=nt, n_blocks=n_blocks),
        grid=(n_blocks + 2,),
        in_specs=[
            pl.BlockSpec(memory_space=pltpu.SMEM),
            pl.BlockSpec((1, TQ, D), lambda s: (cur(s) // nt, cur(s) % nt, 0)),
            pl.BlockSpec((1, TQ, D), lambda s: (prv(s) // nt, prv(s) % nt, 0)),
            _const_spec((1, D)), _const_spec((1, D)),
            _const_spec((D, D_IN)), _const_spec((D // V7X_MXU_DIM, D, V7X_MXU_DIM)),
            _const_spec((CONV_WIDTH, D_RNN)), _const_spec((1, D_RNN)),
            _const_spec((N_GATE_CHUNKS, GATE_CHUNK, 2 * GATE_CHUNK)),
            _const_spec((1, D_RNN)), _const_spec((1, D_RNN)), _const_spec((1, D_RNN)),
            _const_spec((1, D)), _const_spec((1, D)),
            _const_spec((D_FF // FFN_UP_COLS, D, FFN_UP_COLS)), _const_spec((D // V7X_MXU_DIM, D_FF, V7X_MXU_DIM)),
        ],
        out_specs=[
            pl.BlockSpec((1, TQ, D), lambda s: (ppv(s) // nt, ppv(s) % nt, 0)),
            pl.BlockSpec((1, WINDOW, D_KV), lambda s: (prv(s) // nt, 0, 0)),
            pl.BlockSpec((1, WINDOW, D_KV), lambda s: (prv(s) // nt, 0, 0)),
            pl.BlockSpec((1, CONV_WIDTH - 1, D_RNN), lambda s: (prv(s) // nt, 0, 0)),
            pl.BlockSpec((1, 1, D_RNN), lambda s: (prv(s) // nt, 0, 0)),
        ],
        out_shape=[
            jax.ShapeDtypeStruct((B, T, D), _F32),
            jax.ShapeDtypeStruct((B, WINDOW, D_KV), _F32),
            jax.ShapeDtypeStruct((B, WINDOW, D_KV), _F32),
            jax.ShapeDtypeStruct((B, CONV_WIDTH - 1, D_RNN), _F32),
            jax.ShapeDtypeStruct((B, 1, D_RNN), _F32),
        ],
        scratch_shapes=[
            pltpu.VMEM((2, TQ, D_IN), _F32),
            pltpu.VMEM((2, TQ, D), _F32),
            pltpu.VMEM((2, TQ, D), _BF16),
            pltpu.VMEM((2 * N_KV_HEADS, TQ, D_KV), _BF16),
            pltpu.VMEM((2 * N_KV_HEADS, TQ, D_KV), _BF16),
            pltpu.VMEM((2, V7X_SUBLANES + TQ, D_RNN), _F32),
            pltpu.VMEM((2, V7X_SUBLANES, D_RNN), _F32),
            pltpu.VMEM((TQ, D_FF), _BF16), pltpu.VMEM((TQ, D), _F32),
            pltpu.VMEM((TQ, D_RNN), _F32), pltpu.VMEM((TQ, D_RNN), _F32), pltpu.VMEM((TQ, D_RNN), _F32),
        ],
        compiler_params=pltpu.CompilerParams(
            dimension_semantics=("arbitrary",), vmem_limit_bytes=V7X_VMEM_LIMIT_BYTES),
        name="layer_prompt",
    )(sinks, x, x, row(gpre), row(gpost), win_b, wout_b, convw, row(convb), wab, row(ba), row(bx), row(lam),
      row(gpre2), row(gpost2), wup_b, wdn_b)


def _mix_sample_kernel(x_ref, ck_ref, cv_ref, cst_ref, h0_ref, sinks_ref, gpre_ref, gpost_ref, win_ref, wout_ref,
                       convw_ref, convb_ref, wab_ref, ba_ref, bx_ref, lam_ref,
                       x1_ref, kwin_ref, vwin_ref, convst_ref, lru_ref,
                       qrow_ref, orow_ref, knew_ref, vnew_ref, rnn_ref, siga_ref):
    step = pl.program_id(0)
    last = pl.num_programs(0) - 1
    nb = x_ref.shape[0]
    blk = _lane_block((nb, D_KV))

    @pl.when(step == 0)
    def _():
        xnb = _rmsnorm(x_ref[...], gpre_ref[...]).astype(_BF16)
        qkv = _dot(xnb, win_ref[:, OFF_Q:OFF_U])
        knew_ref[...] = qkv[:, OFF_K:OFF_V]
        vnew_ref[...] = qkv[:, OFF_V:OFF_U]
        scale = HEAD_DIM ** -0.5
        for h in range(N_KV_HEADS):
            qc = qkv[:, h * D_KV:(h + 1) * D_KV] * scale
            for g in range(GROUP):
                shift = ((h - g) % GROUP) * HEAD_DIM
                moved = pltpu.roll(qc, shift, axis=1) if shift else qc
                hd = h * GROUP + g
                placed = jnp.where(blk == h, moved, 0.0)
                for half in range(D_KV // V7X_LANES):
                    qrow_ref[half, hd * nb:(hd + 1) * nb, :] = placed[:, half * V7X_LANES:(half + 1) * V7X_LANES]

        u = _dot(xnb, win_ref[:, OFF_U:OFF_G])
        cw = convw_ref[...]
        cs = cst_ref[0] * cw[0:1]
        cs = cs + cst_ref[1] * cw[1:2]
        cs = cs + cst_ref[2] * cw[2:3]
        cs = cs + u * cw[3:4]
        uc = convb_ref[...] + cs
        convst_ref[0] = cst_ref[1]
        convst_ref[1] = cst_ref[2]
        convst_ref[2] = u
        parts = [_lru_gates(uc[:, c * GATE_CHUNK:(c + 1) * GATE_CHUNK], wab_ref[c],
                            *(r[:, c * GATE_CHUNK:(c + 1) * GATE_CHUNK] for r in (ba_ref, bx_ref, lam_ref)))
                 for c in range(N_GATE_CHUNKS)]
        a, mult, i = (jnp.concatenate(p, axis=1) for p in zip(*parts))
        hnew = a * h0_ref[...] + mult * i * uc
        lru_ref[...] = hnew
        g_branch = _dot(xnb, win_ref[:, OFF_G:OFF_GA])
        gate_r = _dot(xnb, win_ref[:, OFF_GR:D_IN])
        rnn_ref[...] = jax.nn.sigmoid(gate_r) * (jax.nn.gelu(g_branch) * hnew)
        siga_ref[...] = jax.nn.sigmoid(_dot(xnb, win_ref[:, OFF_GA:OFF_GR]))

    sink = sinks_ref[...]
    rowid = lax.broadcasted_iota(jnp.int32, (WINDOW, D_KV), 0)

    def per_row(bl, carry):
        b = step * SAMPLE_BC + bl
        kb = ck_ref[bl]
        vb = cv_ref[bl]
        kn = knew_ref[pl.ds(b, 1), :]
        vn = vnew_ref[pl.ds(b, 1), :]
        kwin_ref[bl] = jnp.where(rowid == WINDOW - 1, kn, pltpu.roll(kb, WINDOW - 1, axis=0))
        vwin_ref[bl] = jnp.where(rowid == WINDOW - 1, vn, pltpu.roll(vb, WINDOW - 1, axis=0))

        head_rows = pl.ds(b, N_HEADS, stride=nb)
        qm = jnp.concatenate([qrow_ref[half, head_rows, :] for half in range(D_KV // V7X_LANES)],
                             axis=1).astype(_BF16)
        s = _dot_nt(qm, kb.astype(_BF16))
        knr = kn.astype(_BF16).astype(_F32)
        s_self = jnp.sum(qm.astype(_F32) * knr, axis=-1, keepdims=True)
        m = jnp.maximum(jnp.maximum(jnp.max(s, axis=-1, keepdims=True), s_self), sink)
        p = jnp.exp(s - m)
        p_self = jnp.exp(s_self - m)
        den = jnp.sum(p, axis=-1, keepdims=True) + p_self + jnp.exp(sink - m)
        o = _dot(p.astype(_BF16), vb.astype(_BF16))
        o = o + p_self.astype(_BF16).astype(_F32) * vn.astype(_BF16).astype(_F32)
        o = o * (1.0 / den)
        for half in range(D_KV // V7X_LANES):
            orow_ref[half, head_rows, :] = o[:, half * V7X_LANES:(half + 1) * V7X_LANES]
        return carry

    lax.fori_loop(0, SAMPLE_BC, per_row, 0)

    @pl.when(step == last)
    def _():
        chunks = []
        for h in range(N_KV_HEADS):
            acc = None
            for g in range(GROUP):
                hd = h * GROUP + g
                rows = jnp.concatenate([orow_ref[half, hd * nb:(hd + 1) * nb, :]
                                        for half in range(D_KV // V7X_LANES)], axis=1)
                sel = jnp.where(blk == h, rows, 0.0)
                shift = ((g - h) % GROUP) * HEAD_DIM
                moved = pltpu.roll(sel, shift, axis=1) if shift else sel
                acc = moved if acc is None else acc + moved
            chunks.append(acc)
        attn = jnp.concatenate(chunks, axis=1)
        merged = (siga_ref[...] * attn + rnn_ref[...]).astype(_BF16)
        mix = _dot_col_tiles(merged, wout_ref)
        x1_ref[...] = x_ref[...] + _rmsnorm(mix, gpost_ref[...])


def _mix_sample(x, ck, cv, cst, h0, sinks, gpre, gpost, win_b, wout_b, convw, convb, wab, ba, bx, lam):
    nb, D = x.shape
    row = lambda a: a.reshape(1, -1)
    cache_spec = pl.BlockSpec((SAMPLE_BC, WINDOW, D_KV), lambda s: (s, 0, 0))
    return pl.pallas_call(
        _mix_sample_kernel,
        grid=(nb // SAMPLE_BC,),
        in_specs=[
            _const_spec((nb, D)), cache_spec, cache_spec,
            _const_spec((CONV_WIDTH - 1, nb, D_RNN)), _const_spec((nb, D_RNN)),
            _const_spec((N_HEADS, 1)),
            _const_spec((1, D)), _const_spec((1, D)),
            _const_spec((D, D_IN)), _const_spec((D // V7X_MXU_DIM, D, V7X_MXU_DIM)),
            _const_spec((CONV_WIDTH, D_RNN)), _const_spec((1, D_RNN)),
            _const_spec((N_GATE_CHUNKS, GATE_CHUNK, 2 * GATE_CHUNK)),
            _const_spec((1, D_RNN)), _const_spec((1, D_RNN)), _const_spec((1, D_RNN)),
        ],
        out_specs=[
            pl.BlockSpec((nb, D), lambda s: (0, 0)),
            cache_spec, cache_spec,
            pl.BlockSpec((CONV_WIDTH - 1, nb, D_RNN), lambda s: (0, 0, 0)),
            pl.BlockSpec((nb, D_RNN), lambda s: (0, 0)),
        ],
        out_shape=[
            jax.ShapeDtypeStruct((nb, D), _F32),
            jax.ShapeDtypeStruct((nb, WINDOW, D_KV), _F32),
            jax.ShapeDtypeStruct((nb, WINDOW, D_KV), _F32),
            jax.ShapeDtypeStruct((CONV_WIDTH - 1, nb, D_RNN), _F32),
            jax.ShapeDtypeStruct((nb, D_RNN), _F32),
        ],
        scratch_shapes=[
            pltpu.VMEM((D_KV // V7X_LANES, N_HEADS * nb, V7X_LANES), _F32),
            pltpu.VMEM((D_KV // V7X_LANES, N_HEADS * nb, V7X_LANES), _F32),
            pltpu.VMEM((nb, D_KV), _F32), pltpu.VMEM((nb, D_KV), _F32),
            pltpu.VMEM((nb, D_RNN), _F32), pltpu.VMEM((nb, D), _F32),
        ],
        compiler_params=pltpu.CompilerParams(
            dimension_semantics=("arbitrary",), vmem_limit_bytes=V7X_VMEM_LIMIT_BYTES),
        name="mix_sample",
    )(x, ck, cv, cst, h0, sinks.reshape(N_HEADS, 1), row(gpre), row(gpost), win_b, wout_b, convw, row(convb),
      wab, row(ba), row(bx), row(lam))


def _ffn_kernel(x_ref, gpre_ref, gpost_ref, wup_ref, wdn_ref, o_ref):
    x = x_ref[...]
    hn = _rmsnorm(x, gpre_ref[...]).astype(_BF16)
    up = _dot_col_tiles(hn, wup_ref)
    act = jnp.square(jnp.maximum(up, 0.0)).astype(_BF16)
    f = _dot_col_tiles(act, wdn_ref)
    o_ref[...] = x + _rmsnorm(f, gpost_ref[...])


def _ffn(x, gpre, gpost, wup_b, wdn_b, tm):
    M, D = x.shape
    row = lambda a: a.reshape(1, -1)
    return pl.pallas_call(
        _ffn_kernel,
        grid=(M // tm,),
        in_specs=[
            pl.BlockSpec((tm, D), lambda i: (i, 0)),
            _const_spec((1, D)), _const_spec((1, D)),
            _const_spec((D_FF // FFN_UP_COLS, D, FFN_UP_COLS)), _const_spec((D // V7X_MXU_DIM, D_FF, V7X_MXU_DIM)),
        ],
        out_specs=pl.BlockSpec((tm, D), lambda i: (i, 0)),
        out_shape=jax.ShapeDtypeStruct((M, D), _F32),
        compiler_params=pltpu.CompilerParams(
            dimension_semantics=("arbitrary",), vmem_limit_bytes=V7X_VMEM_LIMIT_BYTES),
        name="ffn",
    )(x, row(gpre), row(gpost), wup_b, wdn_b)


def _gate_weights(w_a, w_x):
    per = GATE_CHUNK // RNN_BLOCK
    eye = jnp.eye(per, dtype=bool)[None, :, None, :, None]

    def bd(w):
        w = w.reshape(N_GATE_CHUNKS, per, RNN_BLOCK, 1, RNN_BLOCK)
        return jnp.where(eye, w, 0.0).reshape(N_GATE_CHUNKS, GATE_CHUNK, GATE_CHUNK)

    return jnp.concatenate([bd(w_a), bd(w_x)], axis=-1).astype(_BF16)


def kernel(x_prompt, x_sample, cache_k_win, cache_v_win, state_conv, state_lru, w_in, w_out, sinks, conv_w, conv_b,
           lru_w_a, lru_b_a, lru_w_x, lru_b_x, lru_lambda, w_up, w_down, g_pre_mix, g_post_mix, g_pre_ffn,
           g_post_ffn):
    B, T, D = x_prompt.shape
    DB, DT, _ = x_sample.shape
    depth = w_in.shape[0]
    assert depth == 1 and DT == 1 and T % (2 * TQ) == 0 and DB % SAMPLE_BC == 0
    l = 0
    win_b = w_in[l].astype(_BF16)
    wout_b = _col_tiles(w_out[l], V7X_MXU_DIM)
    wup_b = _col_tiles(w_up[l], FFN_UP_COLS)
    wdn_b = _col_tiles(w_down[l], V7X_MXU_DIM)
    wab = _gate_weights(lru_w_a[l], lru_w_x[l])
    layer = (g_pre_mix[l], g_post_mix[l], win_b, wout_b, conv_w[l], conv_b[l], wab, lru_b_a[l], lru_b_x[l],
             lru_lambda[l])

    yp, kp, vp, cp, hp = _layer_prompt(x_prompt, sinks[l], *layer, g_pre_ffn[l], g_post_ffn[l], wup_b, wdn_b)

    x1s, kq, vq, cq, hq = _mix_sample(
        x_sample.reshape(DB, D), cache_k_win[l].reshape(DB, WINDOW, D_KV), cache_v_win[l].reshape(DB, WINDOW, D_KV),
        jnp.swapaxes(state_conv[l], 0, 1), state_lru[l], sinks[l], *layer)
    ys = _ffn(x1s, g_pre_ffn[l], g_post_ffn[l], wup_b, wdn_b, DB).reshape(DB, DT, D)

    kv_shape = (1, -1, WINDOW, N_KV_HEADS, HEAD_DIM)
    return (yp, ys, kp.reshape(kv_shape), vp.reshape(kv_shape), cp[None], hp.reshape(1, B, D_RNN),
            kq.reshape(kv_shape), vq.reshape(kv_shape), jnp.swapaxes(cq, 0, 1)[None], hq[None])
```

```python
import functools

import jax
import jax.numpy as jnp
from jax import lax
from jax.experimental import pallas as pl
from jax.experimental.pallas import tpu as pltpu

D_MODEL = 1024
N_HEADS = 16
N_KV_HEADS = 4
HEAD_DIM = 64
GROUP = N_HEADS // N_KV_HEADS
D_ATTN = N_HEADS * HEAD_DIM
D_KV = N_KV_HEADS * HEAD_DIM
WINDOW = 128
D_RNN = D_MODEL
N_RNN_BLOCKS = 16
RNN_BLOCK = D_RNN // N_RNN_BLOCKS
CONV_WIDTH = 4
LRU_C = 8.0
D_FF = 4 * D_MODEL
EPS = 1e-6

OFF_Q = 0
OFF_K = D_ATTN
OFF_V = D_ATTN + D_KV
OFF_U = D_ATTN + 2 * D_KV
OFF_G = OFF_U + D_RNN
OFF_GA = OFF_G + D_RNN
OFF_GR = OFF_GA + D_MODEL
D_IN = OFF_GR + D_MODEL

V7X_SUBLANES = 8
V7X_LANES = 128
V7X_MXU_DIM = 256
V7X_VMEM_LIMIT_BYTES = 60 * 1024 * 1024

TQ = WINDOW
GATE_CHUNK = V7X_MXU_DIM
N_GATE_CHUNKS = D_RNN // GATE_CHUNK
SAMPLE_BC = 8
FFN_UP_COLS = 512

_BF16 = jnp.bfloat16
_F32 = jnp.float32


def _dot(a, b):
    return jnp.dot(a, b, preferred_element_type=_F32)


def _dot_nt(a, b):
    return lax.dot_general(a, b, (((1,), (1,)), ((), ())), preferred_element_type=_F32)


def _dot_col_tiles(a, w_ref):
    return jnp.concatenate([_dot(a, w_ref[j]) for j in range(w_ref.shape[0])], axis=1)


def _col_tiles(w, cols):
    k, n = w.shape
    return jnp.transpose(w.astype(_BF16).reshape(k, n // cols, cols), (1, 0, 2))


def _rmsnorm(x, g):
    return x * lax.rsqrt(jnp.mean(x * x, axis=-1, keepdims=True) + EPS) * g


def _softplus(y):
    return jnp.maximum(y, 0.0) + jnp.log1p(jnp.exp(-jnp.abs(y)))


def _lane_block(shape):
    return lax.broadcasted_iota(jnp.int32, shape, len(shape) - 1) // HEAD_DIM


def _lru_gates(uc, wab, ba, bx, lam):
    gc = _dot(uc.astype(_BF16), wab)
    r = jax.nn.sigmoid(gc[:, :GATE_CHUNK] + ba)
    i = jax.nn.sigmoid(gc[:, GATE_CHUNK:] + bx)
    log_a = -LRU_C * r * _softplus(-lam)
    a = jnp.exp(log_a)
    m2 = -jnp.tanh(log_a) * (a * a + 1.0)
    mult = jnp.where(m2 > 0.0, m2 * lax.rsqrt(m2), 0.0)
    return a, mult, i


def _replicate_head_blocks(m):
    lo, hi = m[:, :V7X_LANES], m[:, V7X_LANES:]
    rlo = pltpu.roll(lo, HEAD_DIM, axis=1)
    rhi = pltpu.roll(hi, HEAD_DIM, axis=1)
    first = lax.broadcasted_iota(jnp.int32, lo.shape, 1) < HEAD_DIM
    halves = (jnp.where(first, lo, rlo), jnp.where(first, rlo, lo),
              jnp.where(first, hi, rhi), jnp.where(first, rhi, hi))
    return [jnp.concatenate([b, b], axis=1) for b in halves]


N_SCRATCH = 20
ROW_GPRE, ROW_GPOST, ROW_CONVB, ROW_BA, ROW_BX, ROW_LAM, ROW_GPRE2, ROW_GPOST2 = range(8)


def _layer_prompt_kernel(*refs, nt):
    step = pl.program_id(0)
    scratch = refs[-N_SCRATCH:]

    @pl.when(step == 0)
    def _():
        for ref in scratch:
            ref[...] = jnp.zeros(ref.shape, ref.dtype)

    for par in (0, 1):
        _layer_prompt_step(par, step, *refs, nt=nt)


def _layer_prompt_step(par, step, sinks_ref, xa_ref, xb_ref, rows_ref, win_ref, wout_ref, convw_ref,
                       wab_ref, wup_ref, wdn_ref,
                       y_ref, kwin_ref, vwin_ref, convst_ref, lru_ref,
                       z0_ref, z1_ref, x10_ref, x11_ref, hn0_ref, hn1_ref, kprev_ref, vprev_ref, uext_ref, hcarry_ref,
                       *temps, nt):
    act_ref, f_ref, a_ref, b_ref, h_ref = temps[par::2]
    za_ref, zb_ref = (z0_ref, z1_ref) if par == 0 else (z1_ref, z0_ref)
    x1c_ref, x1b_ref = (x10_ref, x11_ref) if par == 0 else (x11_ref, x10_ref)
    hnc_ref, hnb_ref = (hn0_ref, hn1_ref) if par == 0 else (hn1_ref, hn0_ref)
    rows_a = slice(par * TQ, (par + 1) * TQ)
    x_res = xb_ref[0, TQ:2 * TQ, :] if par == 0 else xa_ref[0, 0:TQ, :]
    rows_y = slice(par * TQ, (par + 1) * TQ)

    prow = lambda i, cols=slice(None): rows_ref[i:i + 1, cols]
    xnb = _rmsnorm(xa_ref[0, rows_a, :], prow(ROW_GPRE)).astype(_BF16)

    def a_tile(j):
        c = j * V7X_MXU_DIM
        za_ref[:, c:c + V7X_MXU_DIM] = _dot(xnb, win_ref[:, c:c + V7X_MXU_DIM])

    def c_up(j):
        c = j * FFN_UP_COLS
        up = _dot(hnc_ref[...], wup_ref[j])
        act_ref[:, c:c + FFN_UP_COLS] = jnp.square(jnp.maximum(up, 0.0)).astype(_BF16)

    def c_down(j):
        c = j * V7X_MXU_DIM
        f_ref[:, c:c + V7X_MXU_DIM] = _dot(act_ref[...], wdn_ref[j])

    def c_tail():
        y_ref[0, rows_y, :] = x1c_ref[...] + _rmsnorm(f_ref[...], prow(ROW_GPOST2))

    n_a, n_up, n_dn = D_IN // V7X_MXU_DIM, D_FF // FFN_UP_COLS, D_MODEL // V7X_MXU_DIM
    w_up, w_dn = FFN_UP_COLS // V7X_MXU_DIM, D_FF // D_MODEL
    a_iter = iter([(1, functools.partial(a_tile, j)) for j in range(n_a)])
    pieces = []
    for j in range(n_up):
        pieces.append((w_up, functools.partial(c_up, j)))
        pieces.extend(p for _, p in zip(range(1), a_iter))
    for j in range(n_dn):
        pieces.append((w_dn, functools.partial(c_down, j)))
        pieces.extend(p for _, p in zip(range(1), a_iter))
    pieces.append((0, c_tail))
    pieces.extend(a_iter)
    pieces.reverse()
    credit = [0.0]

    def fill(units):
        credit[0] += units
        while pieces and credit[0] >= 0.5 * pieces[-1][0]:
            weight, emit = pieces.pop()
            emit()
            credit[0] -= weight

    fill(2)

    t = (2 * step + par + nt - 1) % nt
    assert nt % 2 == 0
    rd, wr = 1 - par, par
    base = V7X_SUBLANES

    k = zb_ref[:, OFF_K:OFF_V]
    v = zb_ref[:, OFF_V:OFF_U]
    if par == 0:
        kwin_ref[0] = k
        vwin_ref[0] = v
    kcur = [a.astype(_BF16) for a in _replicate_head_blocks(k)]
    vcur = [a.astype(_BF16) for a in _replicate_head_blocks(v)]
    for h in range(N_KV_HEADS):
        kprev_ref[wr * N_KV_HEADS + h] = kcur[h]
        vprev_ref[wr * N_KV_HEADS + h] = vcur[h]

    qi = lax.broadcasted_iota(jnp.int32, (TQ, 2 * TQ), 0)
    kj = lax.broadcasted_iota(jnp.int32, (TQ, 2 * TQ), 1)
    diff = qi + TQ - kj
    first_key = jnp.where(t > 0, 0, TQ)
    mask = (diff >= 0) & (diff <= WINDOW) & (kj >= first_key)
    blk = _lane_block((TQ, D_KV))
    scale = HEAD_DIM ** -0.5

    def attention_head(h):
        qc = zb_ref[:, h * D_KV:(h + 1) * D_KV] * scale
        qm = jnp.concatenate([jnp.where(blk == g, qc, 0.0).astype(_BF16) for g in range(GROUP)], axis=0)
        kk = jnp.concatenate([kprev_ref[rd * N_KV_HEADS + h], kcur[h]], axis=0)
        vv = jnp.concatenate([vprev_ref[rd * N_KV_HEADS + h], vcur[h]], axis=0)
        sc = _dot_nt(qm, kk)
        ps, invs = [], []
        for g in range(GROUP):
            sg = jnp.where(mask, sc[g * TQ:(g + 1) * TQ], -jnp.inf)
            sink = sinks_ref[h * GROUP + g]
            m = jnp.maximum(jnp.max(sg, axis=-1, keepdims=True), sink)
            p = jnp.exp(sg - m)
            den = jnp.sum(p, axis=-1, keepdims=True) + jnp.exp(sink - m)
            invs.append(1.0 / den)
            ps.append(p.astype(_BF16))
            if g % 2 == 1:
                fill(1)
        o = _dot(jnp.concatenate(ps, axis=0), vv)
        acc = jnp.where(blk == 0, o[0:TQ] * invs[0], 0.0)
        for g in range(1, GROUP):
            acc = acc + jnp.where(blk == g, o[g * TQ:(g + 1) * TQ] * invs[g], 0.0)
        fill(1)
        return acc

    def scan_groups(carry, first, last):
        sub = lax.broadcasted_iota(jnp.int32, (V7X_SUBLANES, D_RNN), 0)
        for j in range(first, last):
            rows = slice(j * V7X_SUBLANES, (j + 1) * V7X_SUBLANES)
            aj = a_ref[rows, :]
            bj = b_ref[rows, :]
            for sft in (1, 2, 4):
                keep = sub >= sft
                bj = jnp.where(keep, aj * pltpu.roll(bj, sft, axis=0) + bj, bj)
                aj = jnp.where(keep, aj * pltpu.roll(aj, sft, axis=0), aj)
            hj = aj * carry + bj
            h_ref[rows, :] = hj
            if j % 2 == 1:
                fill(2 if j % 4 == 1 else 1)
            carry = jnp.broadcast_to(hj[V7X_SUBLANES - 1:V7X_SUBLANES, :], (V7X_SUBLANES, D_RNN))
        return carry

    pos0 = (t * TQ + lax.broadcasted_iota(jnp.int32, (TQ, GATE_CHUNK), 0)) == 0
    for c in range(N_GATE_CHUNKS):
        cols = slice(c * GATE_CHUNK, (c + 1) * GATE_CHUNK)
        u = zb_ref[:, OFF_U + c * GATE_CHUNK:OFF_U + (c + 1) * GATE_CHUNK]
        uext_ref[rd, 0:base, cols] = jnp.where(t > 0, uext_ref[rd, 0:base, cols], 0.0)
        uext_ref[rd, base:base + TQ, cols] = u
        cw = convw_ref[:, cols]
        cs = uext_ref[rd, base - 3:base - 3 + TQ, cols] * cw[0:1]
        cs = cs + uext_ref[rd, base - 2:base - 2 + TQ, cols] * cw[1:2]
        cs = cs + uext_ref[rd, base - 1:base - 1 + TQ, cols] * cw[2:3]
        cs = cs + u * cw[3:4]
        uc = prow(ROW_CONVB, cols) + cs
        if par == 0:
            convst_ref[0, :, cols] = uext_ref[rd, base + TQ - (CONV_WIDTH - 1):base + TQ, cols]
        uext_ref[wr, 0:base, cols] = u[TQ - base:TQ, :]
        fill(2)
        a, mult, i = _lru_gates(uc, wab_ref[c], prow(ROW_BA, cols), prow(ROW_BX, cols), prow(ROW_LAM, cols))
        a_ref[:, cols] = a
        b_ref[:, cols] = jnp.where(pos0, 1.0, mult) * i * uc
        fill(2)

    attn_chunks = [attention_head(0), attention_head(1)]
    n_groups = TQ // V7X_SUBLANES
    carry = scan_groups(jnp.where(t > 0, hcarry_ref[rd], 0.0), 0, n_groups // 2)
    attn_chunks.append(attention_head(2))
    carry = scan_groups(carry, n_groups // 2, n_groups)
    hcarry_ref[wr] = carry
    if par == 0:
        lru_ref[0] = carry[0:1, :]
    attn_chunks.append(attention_head(3))

    merged = []
    for c in range(D_MODEL // V7X_MXU_DIM):
        cols = slice(c * V7X_MXU_DIM, (c + 1) * V7X_MXU_DIM)
        zcol = lambda off: zb_ref[:, off + c * V7X_MXU_DIM:off + (c + 1) * V7X_MXU_DIM]
        rnn_out = jax.nn.gelu(zcol(OFF_G)) * h_ref[:, cols]
        merged.append((jax.nn.sigmoid(zcol(OFF_GA)) * attn_chunks[c]
                       + jax.nn.sigmoid(zcol(OFF_GR)) * rnn_out).astype(_BF16))
        fill(1)
    mix = _dot_col_tiles(jnp.concatenate(merged, axis=1), wout_ref)
    fill(3)
    x1 = x_res + _rmsnorm(mix, prow(ROW_GPOST))
    x1b_ref[...] = x1
    hnb_ref[...] = _rmsnorm(x1, prow(ROW_GPRE2)).astype(_BF16)
    fill(sum(w for w, _ in pieces) + len(pieces))
    assert not pieces


def _const_spec(shape):
    zeros = (0,) * len(shape)
    return pl.BlockSpec(shape, lambda *_: zeros, pipeline_mode=pl.Buffered(1))


def _layer_prompt(x, sinks, gpre, gpost, win_b, wout_b, convw, convb, wab, ba, bx, lam, gpre2, gpost2, wup_b, wdn_b):
    B, T, D = x.shape
    nt = T // TQ
    n_blocks = B * nt
    rows = jnp.stack([gpre, gpost, convb, ba, bx, lam, gpre2, gpost2])
    n_pairs = n_blocks // 2
    ntp = nt // 2
    cur = lambda s: jnp.minimum(s, n_pairs - 1)
    prv = lambda s: jnp.maximum(s - 1, 0)
    row_b = lambda s: jnp.maximum(2 * s - 1, 0) // nt
    return pl.pallas_call(
        functools.partial(_layer_prompt_kernel, nt=nt),
        grid=(n_pairs + 1,),
        in_specs=[
            pl.BlockSpec(memory_space=pltpu.SMEM),
            pl.BlockSpec((1, 2 * TQ, D), lambda s: (cur(s) // ntp, cur(s) % ntp, 0)),
            pl.BlockSpec((1, 2 * TQ, D), lambda s: (prv(s) // ntp, prv(s) % ntp, 0)),
            _const_spec((8, D)),
            _const_spec((D, D_IN)), _const_spec((D // V7X_MXU_DIM, D, V7X_MXU_DIM)),
            _const_spec((CONV_WIDTH, D_RNN)),
            _const_spec((N_GATE_CHUNKS, GATE_CHUNK, 2 * GATE_CHUNK)),
            _const_spec((D_FF // FFN_UP_COLS, D, FFN_UP_COLS)), _const_spec((D // V7X_MXU_DIM, D_FF, V7X_MXU_DIM)),
        ],
        out_specs=[
            pl.BlockSpec((1, 2 * TQ, D), lambda s: (prv(s) // ntp, prv(s) % ntp, 0)),
            pl.BlockSpec((1, WINDOW, D_KV), lambda s: (row_b(s), 0, 0)),
            pl.BlockSpec((1, WINDOW, D_KV), lambda s: (row_b(s), 0, 0)),
            pl.BlockSpec((1, CONV_WIDTH - 1, D_RNN), lambda s: (row_b(s), 0, 0)),
            pl.BlockSpec((1, 1, D_RNN), lambda s: (row_b(s), 0, 0)),
        ],
        out_shape=[
            jax.ShapeDtypeStruct((B, T, D), _F32),
            jax.ShapeDtypeStruct((B, WINDOW, D_KV), _F32),
            jax.ShapeDtypeStruct((B, WINDOW, D_KV), _F32),
            jax.ShapeDtypeStruct((B, CONV_WIDTH - 1, D_RNN), _F32),
            jax.ShapeDtypeStruct((B, 1, D_RNN), _F32),
        ],
        scratch_shapes=[
            pltpu.VMEM((TQ, D_IN), _F32), pltpu.VMEM((TQ, D_IN), _F32),
            pltpu.VMEM((TQ, D), _F32), pltpu.VMEM((TQ, D), _F32),
            pltpu.VMEM((TQ, D), _BF16), pltpu.VMEM((TQ, D), _BF16),
            pltpu.VMEM((2 * N_KV_HEADS, TQ, D_KV), _BF16),
            pltpu.VMEM((2 * N_KV_HEADS, TQ, D_KV), _BF16),
            pltpu.VMEM((2, V7X_SUBLANES + TQ, D_RNN), _F32),
            pltpu.VMEM((2, V7X_SUBLANES, D_RNN), _F32),
            *[pltpu.VMEM(shape, dtype) for shape, dtype in
              (((TQ, D_FF), _BF16), ((TQ, D), _F32), ((TQ, D_RNN), _F32), ((TQ, D_RNN), _F32), ((TQ, D_RNN), _F32))
              for _ in range(2)],
        ],
        compiler_params=pltpu.CompilerParams(
            dimension_semantics=("arbitrary",), vmem_limit_bytes=V7X_VMEM_LIMIT_BYTES),
        name="layer_prompt",
    )(sinks, x, x, rows, win_b, wout_b, convw, wab, wup_b, wdn_b)


def _mix_sample_kernel(x_ref, ck_ref, cv_ref, cst_ref, h0_ref, sinks_ref, gpre_ref, gpost_ref, win_ref, wout_ref,
                       convw_ref, convb_ref, wab_ref, ba_ref, bx_ref, lam_ref,
                       x1_ref, kwin_ref, vwin_ref, convst_ref, lru_ref,
                       qrow_ref, orow_ref, knew_ref, vnew_ref, rnn_ref, siga_ref):
    step = pl.program_id(0)
    last = pl.num_programs(0) - 1
    nb = x_ref.shape[0]
    blk = _lane_block((nb, D_KV))

    @pl.when(step == 0)
    def _():
        xnb = _rmsnorm(x_ref[...], gpre_ref[...]).astype(_BF16)
        qkv = _dot(xnb, win_ref[:, OFF_Q:OFF_U])
        knew_ref[...] = qkv[:, OFF_K:OFF_V]
        vnew_ref[...] = qkv[:, OFF_V:OFF_U]
        scale = HEAD_DIM ** -0.5
        for h in range(N_KV_HEADS):
            qc = qkv[:, h * D_KV:(h + 1) * D_KV] * scale
            for g in range(GROUP):
                shift = ((h - g) % GROUP) * HEAD_DIM
                moved = pltpu.roll(qc, shift, axis=1) if shift else qc
                hd = h * GROUP + g
                placed = jnp.where(blk == h, moved, 0.0)
                for half in range(D_KV // V7X_LANES):
                    qrow_ref[half, hd * nb:(hd + 1) * nb, :] = placed[:, half * V7X_LANES:(half + 1) * V7X_LANES]

        u = _dot(xnb, win_ref[:, OFF_U:OFF_G])
        cw = convw_ref[...]
        cs = cst_ref[0] * cw[0:1]
        cs = cs + cst_ref[1] * cw[1:2]
        cs = cs + cst_ref[2] * cw[2:3]
        cs = cs + u * cw[3:4]
        uc = convb_ref[...] + cs
        convst_ref[0] = cst_ref[1]
        convst_ref[1] = cst_ref[2]
        convst_ref[2] = u
        parts = [_lru_gates(uc[:, c * GATE_CHUNK:(c + 1) * GATE_CHUNK], wab_ref[c],
                            *(r[:, c * GATE_CHUNK:(c + 1) * GATE_CHUNK] for r in (ba_ref, bx_ref, lam_ref)))
                 for c in range(N_GATE_CHUNKS)]
        a, mult, i = (jnp.concatenate(p, axis=1) for p in zip(*parts))
        hnew = a * h0_ref[...] + mult * i * uc
        lru_ref[...] = hnew
        g_branch = _dot(xnb, win_ref[:, OFF_G:OFF_GA])
        gate_r = _dot(xnb, win_ref[:, OFF_GR:D_IN])
        rnn_ref[...] = jax.nn.sigmoid(gate_r) * (jax.nn.gelu(g_branch) * hnew)
        siga_ref[...] = jax.nn.sigmoid(_dot(xnb, win_ref[:, OFF_GA:OFF_GR]))

    sink = sinks_ref[...]
    rowid = lax.broadcasted_iota(jnp.int32, (WINDOW, D_KV), 0)

    def per_row(bl, carry):
        b = step * SAMPLE_BC + bl
        kb = ck_ref[bl]
        vb = cv_ref[bl]
        kn = knew_ref[pl.ds(b, 1), :]
        vn = vnew_ref[pl.ds(b, 1), :]
        kwin_ref[bl] = jnp.where(rowid == WINDOW - 1, kn, pltpu.roll(kb, WINDOW - 1, axis=0))
        vwin_ref[bl] = jnp.where(rowid == WINDOW - 1, vn, pltpu.roll(vb, WINDOW - 1, axis=0))

        head_rows = pl.ds(b, N_HEADS, stride=nb)
        qm = jnp.concatenate([qrow_ref[half, head_rows, :] for half in range(D_KV // V7X_LANES)],
                             axis=1).astype(_BF16)
        s = _dot_nt(qm, kb.astype(_BF16))
        knr = kn.astype(_BF16).astype(_F32)
        s_self = jnp.sum(qm.astype(_F32) * knr, axis=-1, keepdims=True)
        m = jnp.maximum(jnp.maximum(jnp.max(s, axis=-1, keepdims=True), s_self), sink)
        p = jnp.exp(s - m)
        p_self = jnp.exp(s_self - m)
        den = jnp.sum(p, axis=-1, keepdims=True) + p_self + jnp.exp(sink - m)
        o = _dot(p.astype(_BF16), vb.astype(_BF16))
        o = o + p_self.astype(_BF16).astype(_F32) * vn.astype(_BF16).astype(_F32)
        o = o * (1.0 / den)
        for half in range(D_KV // V7X_LANES):
            orow_ref[half, head_rows, :] = o[:, half * V7X_LANES:(half + 1) * V7X_LANES]
        return carry

    lax.fori_loop(0, SAMPLE_BC, per_row, 0)

    @pl.when(step == last)
    def _():
        chunks = []
        for h in range(N_KV_HEADS):
            acc = None
            for g in range(GROUP):
                hd = h * GROUP + g
                rows = jnp.concatenate([orow_ref[half, hd * nb:(hd + 1) * nb, :]
                                        for half in range(D_KV // V7X_LANES)], axis=1)
                sel = jnp.where(blk == h, rows, 0.0)
                shift = ((g - h) % GROUP) * HEAD_DIM
                moved = pltpu.roll(sel, shift, axis=1) if shift else sel
                acc = moved if acc is None else acc + moved
            chunks.append(acc)
        attn = jnp.concatenate(chunks, axis=1)
        merged = (siga_ref[...] * attn + rnn_ref[...]).astype(_BF16)
        mix = _dot_col_tiles(merged, wout_ref)
        x1_ref[...] = x_ref[...] + _rmsnorm(mix, gpost_ref[...])


def _mix_sample(x, ck, cv, cst, h0, sinks, gpre, gpost, win_b, wout_b, convw, convb, wab, ba, bx, lam):
    nb, D = x.shape
    row = lambda a: a.reshape(1, -1)
    cache_spec = pl.BlockSpec((SAMPLE_BC, WINDOW, D_KV), lambda s: (s, 0, 0))
    return pl.pallas_call(
        _mix_sample_kernel,
        grid=(nb // SAMPLE_BC,),
        in_specs=[
            _const_spec((nb, D)), cache_spec, cache_spec,
            _const_spec((CONV_WIDTH - 1, nb, D_RNN)), _const_spec((nb, D_RNN)),
            _const_spec((N_HEADS, 1)),
            _const_spec((1, D)), _const_spec((1, D)),
            _const_spec((D, D_IN)), _const_spec((D // V7X_MXU_DIM, D, V7X_MXU_DIM)),
            _const_spec((CONV_WIDTH, D_RNN)), _const_spec((1, D_RNN)),
            _const_spec((N_GATE_CHUNKS, GATE_CHUNK, 2 * GATE_CHUNK)),
            _const_spec((1, D_RNN)), _const_spec((1, D_RNN)), _const_spec((1, D_RNN)),
        ],
        out_specs=[
            pl.BlockSpec((nb, D), lambda s: (0, 0)),
            cache_spec, cache_spec,
            pl.BlockSpec((CONV_WIDTH - 1, nb, D_RNN), lambda s: (0, 0, 0)),
            pl.BlockSpec((nb, D_RNN), lambda s: (0, 0)),
        ],
        out_shape=[
            jax.ShapeDtypeStruct((nb, D), _F32),
            jax.ShapeDtypeStruct((nb, WINDOW, D_KV), _F32),
            jax.ShapeDtypeStruct((nb, WINDOW, D_KV), _F32),
            jax.ShapeDtypeStruct((CONV_WIDTH - 1, nb, D_RNN), _F32),
            jax.ShapeDtypeStruct((nb, D_RNN), _F32),
        ],
        scratch_shapes=[
            pltpu.VMEM((D_KV // V7X_LANES, N_HEADS * nb, V7X_LANES), _F32),
            pltpu.VMEM((D_KV // V7X_LANES, N_HEADS * nb, V7X_LANES), _F32),
            pltpu.VMEM((nb, D_KV), _F32), pltpu.VMEM((nb, D_KV), _F32),
            pltpu.VMEM((nb, D_RNN), _F32), pltpu.VMEM((nb, D), _F32),
        ],
        compiler_params=pltpu.CompilerParams(
            dimension_semantics=("arbitrary",), vmem_limit_bytes=V7X_VMEM_LIMIT_BYTES),
        name="mix_sample",
    )(x, ck, cv, cst, h0, sinks.reshape(N_HEADS, 1), row(gpre), row(gpost), win_b, wout_b, convw, row(convb),
      wab, row(ba), row(bx), row(lam))


def _ffn_kernel(x_ref, gpre_ref, gpost_ref, wup_ref, wdn_ref, o_ref):
    x = x_ref[...]
    hn = _rmsnorm(x, gpre_ref[...]).astype(_BF16)
    up = _dot_col_tiles(hn, wup_ref)
    act = jnp.square(jnp.maximum(up, 0.0)).astype(_BF16)
    f = _dot_col_tiles(act, wdn_ref)
    o_ref[...] = x + _rmsnorm(f, gpost_ref[...])


def _ffn(x, gpre, gpost, wup_b, wdn_b, tm):
    M, D = x.shape
    row = lambda a: a.reshape(1, -1)
    return pl.pallas_call(
        _ffn_kernel,
        grid=(M // tm,),
        in_specs=[
            pl.BlockSpec((tm, D), lambda i: (i, 0)),
            _const_spec((1, D)), _const_spec((1, D)),
            _const_spec((D_FF // FFN_UP_COLS, D, FFN_UP_COLS)), _const_spec((D // V7X_MXU_DIM, D_FF, V7X_MXU_DIM)),
        ],
        out_specs=pl.BlockSpec((tm, D), lambda i: (i, 0)),
        out_shape=jax.ShapeDtypeStruct((M, D), _F32),
        compiler_params=pltpu.CompilerParams(
            dimension_semantics=("arbitrary",), vmem_limit_bytes=V7X_VMEM_LIMIT_BYTES),
        name="ffn",
    )(x, row(gpre), row(gpost), wup_b, wdn_b)


def _gate_weights(w_a, w_x):
    per = GATE_CHUNK // RNN_BLOCK
    eye = jnp.eye(per, dtype=bool)[None, :, None, :, None]

    def bd(w):
        w = w.reshape(N_GATE_CHUNKS, per, RNN_BLOCK, 1, RNN_BLOCK)
        return jnp.where(eye, w, 0.0).reshape(N_GATE_CHUNKS, GATE_CHUNK, GATE_CHUNK)

    return jnp.concatenate([bd(w_a), bd(w_x)], axis=-1).astype(_BF16)


def kernel(x_prompt, x_sample, cache_k_win, cache_v_win, state_conv, state_lru, w_in, w_out, sinks, conv_w, conv_b,
           lru_w_a, lru_b_a, lru_w_x, lru_b_x, lru_lambda, w_up, w_down, g_pre_mix, g_post_mix, g_pre_ffn,
           g_post_ffn):
    B, T, D = x_prompt.shape
    DB, DT, _ = x_sample.shape
    depth = w_in.shape[0]
    assert depth == 1 and DT == 1 and T % (2 * TQ) == 0 and DB % SAMPLE_BC == 0
    l = 0
    win_b = w_in[l].astype(_BF16)
    wout_b = _col_tiles(w_out[l], V7X_MXU_DIM)
    wup_b = _col_tiles(w_up[l], FFN_UP_COLS)
    wdn_b = _col_tiles(w_down[l], V7X_MXU_DIM)
    wab = _gate_weights(lru_w_a[l], lru_w_x[l])
    layer = (g_pre_mix[l], g_post_mix[l], win_b, wout_b, conv_w[l], conv_b[l], wab, lru_b_a[l], lru_b_x[l],
             lru_lambda[l])

    yp, kp, vp, cp, hp = _layer_prompt(x_prompt, sinks[l], *layer, g_pre_ffn[l], g_post_ffn[l], wup_b, wdn_b)

    x1s, kq, vq, cq, hq = _mix_sample(
        x_sample.reshape(DB, D), cache_k_win[l].reshape(DB, WINDOW, D_KV), cache_v_win[l].reshape(DB, WINDOW, D_KV),
        jnp.swapaxes(state_conv[l], 0, 1), state_lru[l], sinks[l], *layer)
    ys = _ffn(x1s, g_pre_ffn[l], g_post_ffn[l], wup_b, wdn_b, DB).reshape(DB, DT, D)

    kv_shape = (1, -1, WINDOW, N_KV_HEADS, HEAD_DIM)
    return (yp, ys, kp.reshape(kv_shape), vp.reshape(kv_shape), cp[None], hp.reshape(1, B, D_RNN),
            kq.reshape(kv_shape), vq.reshape(kv_shape), jnp.swapaxes(cq, 0, 1)[None], hq[None])
```

```python
import functools

import jax
import jax.numpy as jnp
from jax import lax
from jax.experimental import pallas as pl
from jax.experimental.pallas import tpu as pltpu

D_MODEL = 1024
N_HEADS = 16
N_KV_HEADS = 4
HEAD_DIM = 64
GROUP = N_HEADS // N_KV_HEADS
D_ATTN = N_HEADS * HEAD_DIM
D_KV = N_KV_HEADS * HEAD_DIM
WINDOW = 128
D_RNN = D_MODEL
N_RNN_BLOCKS = 16
RNN_BLOCK = D_RNN // N_RNN_BLOCKS
CONV_WIDTH = 4
LRU_C = 8.0
D_FF = 4 * D_MODEL
EPS = 1e-6

OFF_Q = 0
OFF_K = D_ATTN
OFF_V = D_ATTN + D_KV
OFF_U = D_ATTN + 2 * D_KV
OFF_G = OFF_U + D_RNN
OFF_GA = OFF_G + D_RNN
OFF_GR = OFF_GA + D_MODEL
D_IN = OFF_GR + D_MODEL

V7X_SUBLANES = 8
V7X_LANES = 128
V7X_MXU_DIM = 256
V7X_VMEM_LIMIT_BYTES = 60 * 1024 * 1024

TQ = WINDOW
GATE_CHUNK = V7X_MXU_DIM
N_GATE_CHUNKS = D_RNN // GATE_CHUNK
SAMPLE_BC = 8
FFN_UP_COLS = 512

_BF16 = jnp.bfloat16
_F32 = jnp.float32


def _dot(a, b):
    return jnp.dot(a, b, preferred_element_type=_F32)


def _dot_nt(a, b):
    return lax.dot_general(a, b, (((1,), (1,)), ((), ())), preferred_element_type=_F32)


def _dot_col_tiles(a, w_ref):
    return jnp.concatenate([_dot(a, w_ref[j]) for j in range(w_ref.shape[0])], axis=1)


def _col_tiles(w, cols):
    k, n = w.shape
    return jnp.transpose(w.astype(_BF16).reshape(k, n // cols, cols), (1, 0, 2))


def _rmsnorm(x, g):
    return x * lax.rsqrt(jnp.mean(x * x, axis=-1, keepdims=True) + EPS) * g


def _softplus(y):
    return jnp.maximum(y, 0.0) + jnp.log1p(jnp.exp(-jnp.abs(y)))


def _lane_block(shape):
    return lax.broadcasted_iota(jnp.int32, shape, len(shape) - 1) // HEAD_DIM


def _lru_gates(uc, wab, ba, bx, lam):
    gc = _dot(uc.astype(_BF16), wab)
    r = jax.nn.sigmoid(gc[:, :GATE_CHUNK] + ba)
    i = jax.nn.sigmoid(gc[:, GATE_CHUNK:] + bx)
    log_a = -LRU_C * r * _softplus(-lam)
    a = jnp.exp(log_a)
    m2 = -jnp.tanh(log_a) * (a * a + 1.0)
    mult = jnp.where(m2 > 0.0, m2 * lax.rsqrt(m2), 0.0)
    return a, mult, i


def _replicate_head_blocks(m):
    lo, hi = m[:, :V7X_LANES], m[:, V7X_LANES:]
    rlo = pltpu.roll(lo, HEAD_DIM, axis=1)
    rhi = pltpu.roll(hi, HEAD_DIM, axis=1)
    first = lax.broadcasted_iota(jnp.int32, lo.shape, 1) < HEAD_DIM
    halves = (jnp.where(first, lo, rlo), jnp.where(first, rlo, lo),
              jnp.where(first, hi, rhi), jnp.where(first, rhi, hi))
    return [jnp.concatenate([b, b], axis=1) for b in halves]


N_SCRATCH = 20
ROW_GPRE, ROW_GPOST, ROW_CONVB, ROW_BA, ROW_BX, ROW_LAM, ROW_GPRE2, ROW_GPOST2 = range(8)


def _layer_prompt_kernel(*refs, nt):
    step = pl.program_id(0)
    scratch = refs[-N_SCRATCH:]

    @pl.when(step == 0)
    def _():
        for ref in scratch:
            ref[...] = jnp.zeros(ref.shape, ref.dtype)

    for par in (0, 1):
        _layer_prompt_step(par, step, *refs, nt=nt)


def _layer_prompt_step(par, step, sinks_ref, xa_ref, xb_ref, rows_ref, win_ref, wout_ref, convw_ref,
                       wab_ref, wup_ref, wdn_ref,
                       y_ref, kwin_ref, vwin_ref, convst_ref, lru_ref,
                       z0_ref, z1_ref, x10_ref, x11_ref, hn0_ref, hn1_ref, kprev_ref, vprev_ref, uext_ref, hcarry_ref,
                       *temps, nt):
    act_ref, f_ref, a_ref, b_ref, h_ref = temps[par::2]
    za_ref, zb_ref = (z0_ref, z1_ref) if par == 0 else (z1_ref, z0_ref)
    x1c_ref, x1b_ref = (x10_ref, x11_ref) if par == 0 else (x11_ref, x10_ref)
    hnc_ref, hnb_ref = (hn0_ref, hn1_ref) if par == 0 else (hn1_ref, hn0_ref)
    rows_a = slice(par * TQ, (par + 1) * TQ)
    x_res = xb_ref[0, TQ:2 * TQ, :] if par == 0 else xa_ref[0, 0:TQ, :]
    rows_y = slice(par * TQ, (par + 1) * TQ)

    prow = lambda i, cols=slice(None): rows_ref[i:i + 1, cols]
    xnb = _rmsnorm(xa_ref[0, rows_a, :], prow(ROW_GPRE)).astype(_BF16)

    def a_tile(j):
        c = j * V7X_MXU_DIM
        za_ref[:, c:c + V7X_MXU_DIM] = _dot(xnb, win_ref[:, c:c + V7X_MXU_DIM])

    def c_up(j):
        c = j * FFN_UP_COLS
        up = _dot(hnc_ref[...], wup_ref[j])
        act_ref[:, c:c + FFN_UP_COLS] = jnp.square(jnp.maximum(up, 0.0)).astype(_BF16)

    def c_down(j):
        c = j * V7X_MXU_DIM
        f_ref[:, c:c + V7X_MXU_DIM] = _dot(act_ref[...], wdn_ref[j])

    def c_tail():
        y_ref[0, rows_y, :] = x1c_ref[...] + _rmsnorm(f_ref[...], prow(ROW_GPOST2))

    n_a, n_up, n_dn = D_IN // V7X_MXU_DIM, D_FF // FFN_UP_COLS, D_MODEL // V7X_MXU_DIM
    w_up, w_dn = FFN_UP_COLS // V7X_MXU_DIM, D_FF // D_MODEL
    a_iter = iter([(1, functools.partial(a_tile, j)) for j in range(n_a)])
    pieces = []
    for j in range(n_up):
        pieces.append((w_up, functools.partial(c_up, j)))
        pieces.extend(p for _, p in zip(range(1), a_iter))
    for j in range(n_dn):
        pieces.append((w_dn, functools.partial(c_down, j)))
        pieces.extend(p for _, p in zip(range(1), a_iter))
    pieces.append((0, c_tail))
    pieces.extend(a_iter)
    pieces.reverse()
    credit = [0.0]

    def fill(units):
        credit[0] += units
        while pieces and credit[0] >= 0.5 * pieces[-1][0]:
            weight, emit = pieces.pop()
            emit()
            credit[0] -= weight

    fill(2)

    t = (2 * step + par + nt - 1) % nt
    assert nt % 2 == 0
    rd, wr = 1 - par, par
    base = V7X_SUBLANES

    k = zb_ref[:, OFF_K:OFF_V]
    v = zb_ref[:, OFF_V:OFF_U]
    if par == 0:
        kwin_ref[0] = k
        vwin_ref[0] = v
    kcur = [a.astype(_BF16) for a in _replicate_head_blocks(k)]
    vcur = [a.astype(_BF16) for a in _replicate_head_blocks(v)]
    for h in range(N_KV_HEADS):
        kprev_ref[wr * N_KV_HEADS + h] = kcur[h]
        vprev_ref[wr * N_KV_HEADS + h] = vcur[h]

    qi = lax.broadcasted_iota(jnp.int32, (TQ, 2 * TQ), 0)
    kj = lax.broadcasted_iota(jnp.int32, (TQ, 2 * TQ), 1)
    diff = qi + TQ - kj
    first_key = jnp.where(t > 0, 0, TQ)
    mask = (diff >= 0) & (diff <= WINDOW) & (kj >= first_key)
    blk = _lane_block((TQ, D_KV))
    scale = HEAD_DIM ** -0.5

    def attention_head(h):
        qc = zb_ref[:, h * D_KV:(h + 1) * D_KV] * scale
        qm = jnp.concatenate([jnp.where(blk == g, qc, 0.0).astype(_BF16) for g in range(GROUP)], axis=0)
        kk = jnp.concatenate([kprev_ref[rd * N_KV_HEADS + h], kcur[h]], axis=0)
        vv = jnp.concatenate([vprev_ref[rd * N_KV_HEADS + h], vcur[h]], axis=0)
        sc = _dot_nt(qm, kk)
        ps, invs = [], []
        for g in range(GROUP):
            sg = jnp.where(mask, sc[g * TQ:(g + 1) * TQ], -jnp.inf)
            sink = sinks_ref[h * GROUP + g]
            m = jnp.maximum(jnp.max(sg, axis=-1, keepdims=True), sink)
            p = jnp.exp(sg - m)
            den = jnp.sum(p, axis=-1, keepdims=True) + jnp.exp(sink - m)
            invs.append(1.0 / den)
            ps.append(p.astype(_BF16))
            if g % 2 == 1:
                fill(1)
        o = _dot(jnp.concatenate(ps, axis=0), vv)
        acc = jnp.where(blk == 0, o[0:TQ] * invs[0], 0.0)
        for g in range(1, GROUP):
            acc = acc + jnp.where(blk == g, o[g * TQ:(g + 1) * TQ] * invs[g], 0.0)
        fill(1)
        return acc

    def scan_groups(carry, first, last):
        sub = lax.broadcasted_iota(jnp.int32, (V7X_SUBLANES, D_RNN), 0)
        for j in range(first, last):
            rows = slice(j * V7X_SUBLANES, (j + 1) * V7X_SUBLANES)
            aj = a_ref[rows, :]
            bj = b_ref[rows, :]
            for sft in (1, 2, 4):
                keep = sub >= sft
                bj = jnp.where(keep, aj * pltpu.roll(bj, sft, axis=0) + bj, bj)
                aj = jnp.where(keep, aj * pltpu.roll(aj, sft, axis=0), aj)
            hj = aj * carry + bj
            h_ref[rows, :] = hj
            if j % 2 == 1:
                fill(2 if j % 4 == 1 else 1)
            carry = jnp.broadcast_to(hj[V7X_SUBLANES - 1:V7X_SUBLANES, :], (V7X_SUBLANES, D_RNN))
        return carry

    pos0 = (t * TQ + lax.broadcasted_iota(jnp.int32, (TQ, GATE_CHUNK), 0)) == 0
    for c in range(N_GATE_CHUNKS):
        cols = slice(c * GATE_CHUNK, (c + 1) * GATE_CHUNK)
        u = zb_ref[:, OFF_U + c * GATE_CHUNK:OFF_U + (c + 1) * GATE_CHUNK]
        uext_ref[rd, 0:base, cols] = jnp.where(t > 0, uext_ref[rd, 0:base, cols], 0.0)
        uext_ref[rd, base:base + TQ, cols] = u
        cw = convw_ref[:, cols]
        cs = uext_ref[rd, base - 3:base - 3 + TQ, cols] * cw[0:1]
        cs = cs + uext_ref[rd, base - 2:base - 2 + TQ, cols] * cw[1:2]
        cs = cs + uext_ref[rd, base - 1:base - 1 + TQ, cols] * cw[2:3]
        cs = cs + u * cw[3:4]
        uc = prow(ROW_CONVB, cols) + cs
        if par == 0:
            convst_ref[0, :, cols] = uext_ref[rd, base + TQ - (CONV_WIDTH - 1):base + TQ, cols]
        uext_ref[wr, 0:base, cols] = u[TQ - base:TQ, :]
        fill(2)
        a, mult, i = _lru_gates(uc, wab_ref[c], prow(ROW_BA, cols), prow(ROW_BX, cols), prow(ROW_LAM, cols))
        a_ref[:, cols] = a
        b_ref[:, cols] = jnp.where(pos0, 1.0, mult) * i * uc
        fill(2)

    attn_chunks = [attention_head(0), attention_head(1)]
    n_groups = TQ // V7X_SUBLANES
    carry = scan_groups(jnp.where(t > 0, hcarry_ref[rd], 0.0), 0, n_groups // 2)
    attn_chunks.append(attention_head(2))
    carry = scan_groups(carry, n_groups // 2, n_groups)
    hcarry_ref[wr] = carry
    if par == 0:
        lru_ref[0] = carry[0:1, :]
    attn_chunks.append(attention_head(3))

    merged = []
    for c in range(D_MODEL // V7X_MXU_DIM):
        cols = slice(c * V7X_MXU_DIM, (c + 1) * V7X_MXU_DIM)
        zcol = lambda off: zb_ref[:, off + c * V7X_MXU_DIM:off + (c + 1) * V7X_MXU_DIM]
        rnn_out = jax.nn.gelu(zcol(OFF_G)) * h_ref[:, cols]
        merged.append((jax.nn.sigmoid(zcol(OFF_GA)) * attn_chunks[c]
                       + jax.nn.sigmoid(zcol(OFF_GR)) * rnn_out).astype(_BF16))
        fill(1)
    mix = _dot_col_tiles(jnp.concatenate(merged, axis=1), wout_ref)
    fill(3)
    x1 = x_res + _rmsnorm(mix, prow(ROW_GPOST))
    x1b_ref[...] = x1
    hnb_ref[...] = _rmsnorm(x1, prow(ROW_GPRE2)).astype(_BF16)
    fill(sum(w for w, _ in pieces) + len(pieces))
    assert not pieces


def _const_spec(shape):
    zeros = (0,) * len(shape)
    return pl.BlockSpec(shape, lambda *_: zeros, pipeline_mode=pl.Buffered(1))


def _layer_prompt(x, sinks, gpre, gpost, win_b, wout_b, convw, convb, wab, ba, bx, lam, gpre2, gpost2, wup_b, wdn_b):
    B, T, D = x.shape
    nt = T // TQ
    n_blocks = B * nt
    rows = jnp.stack([gpre, gpost, convb, ba, bx, lam, gpre2, gpost2])
    n_pairs = n_blocks // 2
    ntp = nt // 2
    cur = lambda s: jnp.minimum(s, n_pairs - 1)
    prv = lambda s: jnp.maximum(s - 1, 0)
    row_b = lambda s: jnp.maximum(2 * s - 1, 0) // nt
    return pl.pallas_call(
        functools.partial(_layer_prompt_kernel, nt=nt),
        grid=(n_pairs + 1,),
        in_specs=[
            pl.BlockSpec(memory_space=pltpu.SMEM),
            pl.BlockSpec((1, 2 * TQ, D), lambda s: (cur(s) // ntp, cur(s) % ntp, 0)),
            pl.BlockSpec((1, 2 * TQ, D), lambda s: (prv(s) // ntp, prv(s) % ntp, 0)),
            _const_spec((8, D)),
            _const_spec((D, D_IN)), _const_spec((D // V7X_MXU_DIM, D, V7X_MXU_DIM)),
            _const_spec((CONV_WIDTH, D_RNN)),
            _const_spec((N_GATE_CHUNKS, GATE_CHUNK, 2 * GATE_CHUNK)),
            _const_spec((D_FF // FFN_UP_COLS, D, FFN_UP_COLS)), _const_spec((D // V7X_MXU_DIM, D_FF, V7X_MXU_DIM)),
        ],
        out_specs=[
            pl.BlockSpec((1, 2 * TQ, D), lambda s: (prv(s) // ntp, prv(s) % ntp, 0)),
            pl.BlockSpec((1, WINDOW, D_KV), lambda s: (row_b(s), 0, 0)),
            pl.BlockSpec((1, WINDOW, D_KV), lambda s: (row_b(s), 0, 0)),
            pl.BlockSpec((1, CONV_WIDTH - 1, D_RNN), lambda s: (row_b(s), 0, 0)),
            pl.BlockSpec((1, 1, D_RNN), lambda s: (row_b(s), 0, 0)),
        ],
        out_shape=[
            jax.ShapeDtypeStruct((B, T, D), _F32),
            jax.ShapeDtypeStruct((B, WINDOW, D_KV), _F32),
            jax.ShapeDtypeStruct((B, WINDOW, D_KV), _F32),
            jax.ShapeDtypeStruct((B, CONV_WIDTH - 1, D_RNN), _F32),
            jax.ShapeDtypeStruct((B, 1, D_RNN), _F32),
        ],
        scratch_shapes=[
            pltpu.VMEM((TQ, D_IN), _F32), pltpu.VMEM((TQ, D_IN), _F32),
            pltpu.VMEM((TQ, D), _F32), pltpu.VMEM((TQ, D), _F32),
            pltpu.VMEM((TQ, D), _BF16), pltpu.VMEM((TQ, D), _BF16),
            pltpu.VMEM((2 * N_KV_HEADS, TQ, D_KV), _BF16),
            pltpu.VMEM((2 * N_KV_HEADS, TQ, D_KV), _BF16),
            pltpu.VMEM((2, V7X_SUBLANES + TQ, D_RNN), _F32),
            pltpu.VMEM((2, V7X_SUBLANES, D_RNN), _F32),
            *[pltpu.VMEM(shape, dtype) for shape, dtype in
              (((TQ, D_FF), _BF16), ((TQ, D), _F32), ((TQ, D_RNN), _F32), ((TQ, D_RNN), _F32), ((TQ, D_RNN), _F32))
              for _ in range(2)],
        ],
        compiler_params=pltpu.CompilerParams(
            dimension_semantics=("arbitrary",), vmem_limit_bytes=V7X_VMEM_LIMIT_BYTES),
        name="layer_prompt",
    )(sinks, x, x, rows, win_b, wout_b, convw, wab, wup_b, wdn_b)


def _mix_sample_kernel(x_ref, ck_ref, cv_ref, cst_ref, h0_ref, sinks_ref, gpre_ref, gpost_ref, win_ref, wout_ref,
                       convw_ref, convb_ref, wab_ref, ba_ref, bx_ref, lam_ref,
                       x1_ref, kwin_ref, vwin_ref, convst_ref, lru_ref,
                       qrow_ref, orow_ref, knew_ref, vnew_ref, rnn_ref, siga_ref):
    step = pl.program_id(0)
    last = pl.num_programs(0) - 1
    nb = x_ref.shape[0]
    blk = _lane_block((nb, D_KV))

    @pl.when(step == 0)
    def _():
        xnb = _rmsnorm(x_ref[...], gpre_ref[...]).astype(_BF16)
        qkv = _dot(xnb, win_ref[:, OFF_Q:OFF_U])
        knew_ref[...] = qkv[:, OFF_K:OFF_V]
        vnew_ref[...] = qkv[:, OFF_V:OFF_U]
        scale = HEAD_DIM ** -0.5
        for h in range(N_KV_HEADS):
            qc = qkv[:, h * D_KV:(h + 1) * D_KV] * scale
            for g in range(GROUP):
                shift = ((h - g) % GROUP) * HEAD_DIM
                moved = pltpu.roll(qc, shift, axis=1) if shift else qc
                hd = h * GROUP + g
                placed = jnp.where(blk == h, moved, 0.0)
                for half in range(D_KV // V7X_LANES):
                    qrow_ref[half, hd * nb:(hd + 1) * nb, :] = placed[:, half * V7X_LANES:(half + 1) * V7X_LANES]

        u = _dot(xnb, win_ref[:, OFF_U:OFF_G])
        cw = convw_ref[...]
        cs = cst_ref[0] * cw[0:1]
        cs = cs + cst_ref[1] * cw[1:2]
        cs = cs + cst_ref[2] * cw[2:3]
        cs = cs + u * cw[3:4]
        uc = convb_ref[...] + cs
        convst_ref[0] = cst_ref[1]
        convst_ref[1] = cst_ref[2]
        convst_ref[2] = u
        parts = [_lru_gates(uc[:, c * GATE_CHUNK:(c + 1) * GATE_CHUNK], wab_ref[c],
                            *(r[:, c * GATE_CHUNK:(c + 1) * GATE_CHUNK] for r in (ba_ref, bx_ref, lam_ref)))
                 for c in range(N_GATE_CHUNKS)]
        a, mult, i = (jnp.concatenate(p, axis=1) for p in zip(*parts))
        hnew = a * h0_ref[...] + mult * i * uc
        lru_ref[...] = hnew
        g_branch = _dot(xnb, win_ref[:, OFF_G:OFF_GA])
        gate_r = _dot(xnb, win_ref[:, OFF_GR:D_IN])
        rnn_ref[...] = jax.nn.sigmoid(gate_r) * (jax.nn.gelu(g_branch) * hnew)
        siga_ref[...] = jax.nn.sigmoid(_dot(xnb, win_ref[:, OFF_GA:OFF_GR]))

    sink = sinks_ref[...]
    rowid = lax.broadcasted_iota(jnp.int32, (WINDOW, D_KV), 0)

    def per_row(bl, carry):
        b = step * SAMPLE_BC + bl
        kb = ck_ref[bl]
        vb = cv_ref[bl]
        kn = knew_ref[pl.ds(b, 1), :]
        vn = vnew_ref[pl.ds(b, 1), :]
        kwin_ref[bl] = jnp.where(rowid == WINDOW - 1, kn, pltpu.roll(kb, WINDOW - 1, axis=0))
        vwin_ref[bl] = jnp.where(rowid == WINDOW - 1, vn, pltpu.roll(vb, WINDOW - 1, axis=0))

        head_rows = pl.ds(b, N_HEADS, stride=nb)
        qm = jnp.concatenate([qrow_ref[half, head_rows, :] for half in range(D_KV // V7X_LANES)],
                             axis=1).astype(_BF16)
        s = _dot_nt(qm, kb.astype(_BF16))
        knr = kn.astype(_BF16).astype(_F32)
        s_self = jnp.sum(qm.astype(_F32) * knr, axis=-1, keepdims=True)
        m = jnp.maximum(jnp.maximum(jnp.max(s, axis=-1, keepdims=True), s_self), sink)
        p = jnp.exp(s - m)
        p_self = jnp.exp(s_self - m)
        den = jnp.sum(p, axis=-1, keepdims=True) + p_self + jnp.exp(sink - m)
        o = _dot(p.astype(_BF16), vb.astype(_BF16))
        o = o + p_self.astype(_BF16).astype(_F32) * vn.astype(_BF16).astype(_F32)
        o = o * (1.0 / den)
        for half in range(D_KV // V7X_LANES):
            orow_ref[half, head_rows, :] = o[:, half * V7X_LANES:(half + 1) * V7X_LANES]
        return carry

    lax.fori_loop(0, SAMPLE_BC, per_row, 0, unroll=True)

    @pl.when(step == last)
    def _():
        chunks = []
        for h in range(N_KV_HEADS):
            acc = None
            for g in range(GROUP):
                hd = h * GROUP + g
                rows = jnp.concatenate([orow_ref[half, hd * nb:(hd + 1) * nb, :]
                                        for half in range(D_KV // V7X_LANES)], axis=1)
                sel = jnp.where(blk == h, rows, 0.0)
                shift = ((g - h) % GROUP) * HEAD_DIM
                moved = pltpu.roll(sel, shift, axis=1) if shift else sel
                acc = moved if acc is None else acc + moved
            chunks.append(acc)
        attn = jnp.concatenate(chunks, axis=1)
        merged = (siga_ref[...] * attn + rnn_ref[...]).astype(_BF16)
        mix = _dot_col_tiles(merged, wout_ref)
        x1_ref[...] = x_ref[...] + _rmsnorm(mix, gpost_ref[...])


def _mix_sample(x, ck, cv, cst, h0, sinks, gpre, gpost, win_b, wout_b, convw, convb, wab, ba, bx, lam):
    nb, D = x.shape
    row = lambda a: a.reshape(1, -1)
    cache_spec = pl.BlockSpec((SAMPLE_BC, WINDOW, D_KV), lambda s: (s, 0, 0))
    return pl.pallas_call(
        _mix_sample_kernel,
        grid=(nb // SAMPLE_BC,),
        in_specs=[
            _const_spec((nb, D)), cache_spec, cache_spec,
            _const_spec((CONV_WIDTH - 1, nb, D_RNN)), _const_spec((nb, D_RNN)),
            _const_spec((N_HEADS, 1)),
            _const_spec((1, D)), _const_spec((1, D)),
            _const_spec((D, D_IN)), _const_spec((D // V7X_MXU_DIM, D, V7X_MXU_DIM)),
            _const_spec((CONV_WIDTH, D_RNN)), _const_spec((1, D_RNN)),
            _const_spec((N_GATE_CHUNKS, GATE_CHUNK, 2 * GATE_CHUNK)),
            _const_spec((1, D_RNN)), _const_spec((1, D_RNN)), _const_spec((1, D_RNN)),
        ],
        out_specs=[
            pl.BlockSpec((nb, D), lambda s: (0, 0)),
            cache_spec, cache_spec,
            pl.BlockSpec((CONV_WIDTH - 1, nb, D_RNN), lambda s: (0, 0, 0)),
            pl.BlockSpec((nb, D_RNN), lambda s: (0, 0)),
        ],
        out_shape=[
            jax.ShapeDtypeStruct((nb, D), _F32),
            jax.ShapeDtypeStruct((nb, WINDOW, D_KV), _F32),
            jax.ShapeDtypeStruct((nb, WINDOW, D_KV), _F32),
            jax.ShapeDtypeStruct((CONV_WIDTH - 1, nb, D_RNN), _F32),
            jax.ShapeDtypeStruct((nb, D_RNN), _F32),
        ],
        scratch_shapes=[
            pltpu.VMEM((D_KV // V7X_LANES, N_HEADS * nb, V7X_LANES), _F32),
            pltpu.VMEM((D_KV // V7X_LANES, N_HEADS * nb, V7X_LANES), _F32),
            pltpu.VMEM((nb, D_KV), _F32), pltpu.VMEM((nb, D_KV), _F32),
            pltpu.VMEM((nb, D_RNN), _F32), pltpu.VMEM((nb, D), _F32),
        ],
        compiler_params=pltpu.CompilerParams(
            dimension_semantics=("arbitrary",), vmem_limit_bytes=V7X_VMEM_LIMIT_BYTES),
        name="mix_sample",
    )(x, ck, cv, cst, h0, sinks.reshape(N_HEADS, 1), row(gpre), row(gpost), win_b, wout_b, convw, row(convb),
      wab, row(ba), row(bx), row(lam))


def _ffn_kernel(x_ref, gpre_ref, gpost_ref, wup_ref, wdn_ref, o_ref):
    x = x_ref[...]
    hn = _rmsnorm(x, gpre_ref[...]).astype(_BF16)
    up = _dot_col_tiles(hn, wup_ref)
    act = jnp.square(jnp.maximum(up, 0.0)).astype(_BF16)
    f = _dot_col_tiles(act, wdn_ref)
    o_ref[...] = x + _rmsnorm(f, gpost_ref[...])


def _ffn(x, gpre, gpost, wup_b, wdn_b, tm):
    M, D = x.shape
    row = lambda a: a.reshape(1, -1)
    return pl.pallas_call(
        _ffn_kernel,
        grid=(M // tm,),
        in_specs=[
            pl.BlockSpec((tm, D), lambda i: (i, 0)),
            _const_spec((1, D)), _const_spec((1, D)),
            _const_spec((D_FF // FFN_UP_COLS, D, FFN_UP_COLS)), _const_spec((D // V7X_MXU_DIM, D_FF, V7X_MXU_DIM)),
        ],
        out_specs=pl.BlockSpec((tm, D), lambda i: (i, 0)),
        out_shape=jax.ShapeDtypeStruct((M, D), _F32),
        compiler_params=pltpu.CompilerParams(
            dimension_semantics=("arbitrary",), vmem_limit_bytes=V7X_VMEM_LIMIT_BYTES),
        name="ffn",
    )(x, row(gpre), row(gpost), wup_b, wdn_b)


def _gate_weights(w_a, w_x):
    per = GATE_CHUNK // RNN_BLOCK
    eye = jnp.eye(per, dtype=bool)[None, :, None, :, None]

    def bd(w):
        w = w.reshape(N_GATE_CHUNKS, per, RNN_BLOCK, 1, RNN_BLOCK)
        return jnp.where(eye, w, 0.0).reshape(N_GATE_CHUNKS, GATE_CHUNK, GATE_CHUNK)

    return jnp.concatenate([bd(w_a), bd(w_x)], axis=-1).astype(_BF16)


def kernel(x_prompt, x_sample, cache_k_win, cache_v_win, state_conv, state_lru, w_in, w_out, sinks, conv_w, conv_b,
           lru_w_a, lru_b_a, lru_w_x, lru_b_x, lru_lambda, w_up, w_down, g_pre_mix, g_post_mix, g_pre_ffn,
           g_post_ffn):
    B, T, D = x_prompt.shape
    DB, DT, _ = x_sample.shape
    depth = w_in.shape[0]
    assert depth == 1 and DT == 1 and T % (2 * TQ) == 0 and DB % SAMPLE_BC == 0
    l = 0
    win_b = w_in[l].astype(_BF16)
    wout_b = _col_tiles(w_out[l], V7X_MXU_DIM)
    wup_b = _col_tiles(w_up[l], FFN_UP_COLS)
    wdn_b = _col_tiles(w_down[l], V7X_MXU_DIM)
    wab = _gate_weights(lru_w_a[l], lru_w_x[l])
    layer = (g_pre_mix[l], g_post_mix[l], win_b, wout_b, conv_w[l], conv_b[l], wab, lru_b_a[l], lru_b_x[l],
             lru_lambda[l])

    yp, kp, vp, cp, hp = _layer_prompt(x_prompt, sinks[l], *layer, g_pre_ffn[l], g_post_ffn[l], wup_b, wdn_b)

    x1s, kq, vq, cq, hq = _mix_sample(
        x_sample.reshape(DB, D), cache_k_win[l].reshape(DB, WINDOW, D_KV), cache_v_win[l].reshape(DB, WINDOW, D_KV),
        jnp.swapaxes(state_conv[l], 0, 1), state_lru[l], sinks[l], *layer)
    ys = _ffn(x1s, g_pre_ffn[l], g_post_ffn[l], wup_b, wdn_b, DB).reshape(DB, DT, D)

    kv_shape = (1, -1, WINDOW, N_KV_HEADS, HEAD_DIM)
    return (yp, ys, kp.reshape(kv_shape), vp.reshape(kv_shape), cp[None], hp.reshape(1, B, D_RNN),
            kq.reshape(kv_shape), vq.reshape(kv_shape), jnp.swapaxes(cq, 0, 1)[None], hq[None])
```

```python
import functools

import jax
import jax.numpy as jnp
from jax import lax
from jax.experimental import pallas as pl
from jax.experimental.pallas import tpu as pltpu

D_MODEL = 1024
N_HEADS = 16
N_KV_HEADS = 4
HEAD_DIM = 64
GROUP = N_HEADS // N_KV_HEADS
D_ATTN = N_HEADS * HEAD_DIM
D_KV = N_KV_HEADS * HEAD_DIM
WINDOW = 128
D_RNN = D_MODEL
N_RNN_BLOCKS = 16
RNN_BLOCK = D_RNN // N_RNN_BLOCKS
CONV_WIDTH = 4
LRU_C = 8.0
D_FF = 4 * D_MODEL
EPS = 1e-6

OFF_Q = 0
OFF_K = D_ATTN
OFF_V = D_ATTN + D_KV
OFF_U = D_ATTN + 2 * D_KV
OFF_G = OFF_U + D_RNN
OFF_GA = OFF_G + D_RNN
OFF_GR = OFF_GA + D_MODEL
D_IN = OFF_GR + D_MODEL

V7X_SUBLANES = 8
V7X_LANES = 128
V7X_MXU_DIM = 256
V7X_VMEM_LIMIT_BYTES = 60 * 1024 * 1024

TQ = WINDOW
GATE_CHUNK = V7X_MXU_DIM
N_GATE_CHUNKS = D_RNN // GATE_CHUNK
SAMPLE_BC = 8
FFN_UP_COLS = 512

_BF16 = jnp.bfloat16
_F32 = jnp.float32


def _dot(a, b):
    return jnp.dot(a, b, preferred_element_type=_F32)


def _dot_nt(a, b):
    return lax.dot_general(a, b, (((1,), (1,)), ((), ())), preferred_element_type=_F32)


def _dot_col_tiles(a, w_ref):
    return jnp.concatenate([_dot(a, w_ref[j]) for j in range(w_ref.shape[0])], axis=1)


def _col_tiles(w, cols):
    k, n = w.shape
    return jnp.transpose(w.astype(_BF16).reshape(k, n // cols, cols), (1, 0, 2))


def _rmsnorm(x, g):
    return x * lax.rsqrt(jnp.mean(x * x, axis=-1, keepdims=True) + EPS) * g


def _softplus(y):
    return jnp.maximum(y, 0.0) + jnp.log1p(jnp.exp(-jnp.abs(y)))


def _lane_block(shape):
    return lax.broadcasted_iota(jnp.int32, shape, len(shape) - 1) // HEAD_DIM


def _lru_gates(uc, wab, ba, bx, lam):
    gc = _dot(uc.astype(_BF16), wab)
    r = jax.nn.sigmoid(gc[:, :GATE_CHUNK] + ba)
    i = jax.nn.sigmoid(gc[:, GATE_CHUNK:] + bx)
    log_a = -LRU_C * r * _softplus(-lam)
    a = jnp.exp(log_a)
    m2 = -jnp.tanh(log_a) * (a * a + 1.0)
    mult = jnp.where(m2 > 0.0, m2 * lax.rsqrt(m2), 0.0)
    return a, mult, i


def _replicate_head_blocks(m):
    lo, hi = m[:, :V7X_LANES], m[:, V7X_LANES:]
    rlo = pltpu.roll(lo, HEAD_DIM, axis=1)
    rhi = pltpu.roll(hi, HEAD_DIM, axis=1)
    first = lax.broadcasted_iota(jnp.int32, lo.shape, 1) < HEAD_DIM
    halves = (jnp.where(first, lo, rlo), jnp.where(first, rlo, lo),
              jnp.where(first, hi, rhi), jnp.where(first, rhi, hi))
    return [jnp.concatenate([b, b], axis=1) for b in halves]


N_SCRATCH = 20
ROW_GPRE, ROW_GPOST, ROW_CONVB, ROW_BA, ROW_BX, ROW_LAM, ROW_GPRE2, ROW_GPOST2 = range(8)


def _layer_prompt_kernel(*refs, nt):
    step = pl.program_id(0)
    scratch = refs[-N_SCRATCH:]

    @pl.when(step == 0)
    def _():
        for ref in scratch:
            ref[...] = jnp.zeros(ref.shape, ref.dtype)

    for par in (0, 1):
        _layer_prompt_step(par, step, *refs, nt=nt)


def _layer_prompt_step(par, step, sinks_ref, xa_ref, xb_ref, rows_ref, win_ref, wout_ref, convw_ref,
                       wab_ref, wup_ref, wdn_ref,
                       y_ref, kwin_ref, vwin_ref, convst_ref, lru_ref,
                       z0_ref, z1_ref, x10_ref, x11_ref, hn0_ref, hn1_ref, kprev_ref, vprev_ref, uext_ref, hcarry_ref,
                       *temps, nt):
    act_ref, f_ref, a_ref, b_ref, h_ref = temps[par::2]
    za_ref, zb_ref = (z0_ref, z1_ref) if par == 0 else (z1_ref, z0_ref)
    x1c_ref, x1b_ref = (x10_ref, x11_ref) if par == 0 else (x11_ref, x10_ref)
    hnc_ref, hnb_ref = (hn0_ref, hn1_ref) if par == 0 else (hn1_ref, hn0_ref)
    rows_a = slice(par * TQ, (par + 1) * TQ)
    x_res = xb_ref[0, TQ:2 * TQ, :] if par == 0 else xa_ref[0, 0:TQ, :]
    rows_y = slice(par * TQ, (par + 1) * TQ)

    prow = lambda i, cols=slice(None): rows_ref[i:i + 1, cols]
    xnb = _rmsnorm(xa_ref[0, rows_a, :], prow(ROW_GPRE)).astype(_BF16)

    def a_tile(j):
        c = j * V7X_MXU_DIM
        za_ref[:, c:c + V7X_MXU_DIM] = _dot(xnb, win_ref[:, c:c + V7X_MXU_DIM])

    def c_up(j):
        c = j * FFN_UP_COLS
        up = _dot(hnc_ref[...], wup_ref[j])
        act_ref[:, c:c + FFN_UP_COLS] = jnp.square(jnp.maximum(up, 0.0)).astype(_BF16)

    def c_down(j):
        c = j * V7X_MXU_DIM
        f_ref[:, c:c + V7X_MXU_DIM] = _dot(act_ref[...], wdn_ref[j])

    def c_tail():
        y_ref[0, rows_y, :] = x1c_ref[...] + _rmsnorm(f_ref[...], prow(ROW_GPOST2))

    n_a, n_up, n_dn = D_IN // V7X_MXU_DIM, D_FF // FFN_UP_COLS, D_MODEL // V7X_MXU_DIM
    w_up, w_dn = FFN_UP_COLS // V7X_MXU_DIM, D_FF // D_MODEL
    a_iter = iter([(1, functools.partial(a_tile, j)) for j in range(n_a)])
    pieces = []
    for j in range(n_up):
        pieces.append((w_up, functools.partial(c_up, j)))
        pieces.extend(p for _, p in zip(range(1), a_iter))
    for j in range(n_dn):
        pieces.append((w_dn, functools.partial(c_down, j)))
        pieces.extend(p for _, p in zip(range(1), a_iter))
    pieces.append((0, c_tail))
    pieces.extend(a_iter)
    pieces.reverse()
    credit = [0.0]

    def fill(units):
        credit[0] += units
        while pieces and credit[0] >= 0.5 * pieces[-1][0]:
            weight, emit = pieces.pop()
            emit()
            credit[0] -= weight

    fill(2)

    t = (2 * step + par + nt - 1) % nt
    assert nt % 2 == 0
    rd, wr = 1 - par, par
    base = V7X_SUBLANES

    k = zb_ref[:, OFF_K:OFF_V]
    v = zb_ref[:, OFF_V:OFF_U]
    if par == 0:
        kwin_ref[0] = k
        vwin_ref[0] = v
    kcur = [a.astype(_BF16) for a in _replicate_head_blocks(k)]
    vt = jnp.transpose(v).astype(_BF16)
    vcur = [vt[h * HEAD_DIM:(h + 1) * HEAD_DIM, :] for h in range(N_KV_HEADS)]
    for h in range(N_KV_HEADS):
        kprev_ref[wr * N_KV_HEADS + h] = kcur[h]
        vprev_ref[wr * N_KV_HEADS + h] = vcur[h]

    kj = lax.broadcasted_iota(jnp.int32, (2 * TQ, TQ), 0)
    qi = lax.broadcasted_iota(jnp.int32, (2 * TQ, TQ), 1)
    diff = qi + TQ - kj
    first_key = jnp.where(t > 0, 0, TQ)
    mask = (diff >= 0) & (diff <= WINDOW) & (kj >= first_key)
    blk = _lane_block((TQ, D_KV))
    scale = HEAD_DIM ** -0.5

    def attention_head(h):
        qc = zb_ref[:, h * D_KV:(h + 1) * D_KV] * scale
        qm = jnp.concatenate([jnp.where(blk == g, qc, 0.0).astype(_BF16) for g in range(GROUP)], axis=0)
        kk = jnp.concatenate([kprev_ref[rd * N_KV_HEADS + h], kcur[h]], axis=0)
        vv = jnp.concatenate([vprev_ref[rd * N_KV_HEADS + h], vcur[h]], axis=1)
        sc = _dot_nt(kk, qm)
        ps, invs = [], []
        for g in range(GROUP):
            sg = jnp.where(mask, sc[:, g * TQ:(g + 1) * TQ], -jnp.inf)
            sink = sinks_ref[h * GROUP + g]
            m = jnp.maximum(jnp.max(sg, axis=0, keepdims=True), sink)
            p = jnp.exp(sg - m)
            den = jnp.sum(p, axis=0, keepdims=True) + jnp.exp(sink - m)
            invs.append(1.0 / den)
            ps.append(p.astype(_BF16))
            if g % 2 == 1:
                fill(1)
        o = _dot(vv, jnp.concatenate(ps, axis=1))
        o = jnp.concatenate([o[:, g * TQ:(g + 1) * TQ] * invs[g] for g in range(GROUP)], axis=0)
        fill(1)
        return jnp.transpose(o)

    def scan_groups(carry, first, last):
        sub = lax.broadcasted_iota(jnp.int32, (V7X_SUBLANES, D_RNN), 0)
        for j in range(first, last):
            rows = slice(j * V7X_SUBLANES, (j + 1) * V7X_SUBLANES)
            aj = a_ref[rows, :]
            bj = b_ref[rows, :]
            for sft in (1, 2, 4):
                keep = sub >= sft
                bj = jnp.where(keep, aj * pltpu.roll(bj, sft, axis=0) + bj, bj)
                aj = jnp.where(keep, aj * pltpu.roll(aj, sft, axis=0), aj)
            hj = aj * carry + bj
            h_ref[rows, :] = hj
            if j % 2 == 1:
                fill(2 if j % 4 == 1 else 1)
            carry = jnp.broadcast_to(hj[V7X_SUBLANES - 1:V7X_SUBLANES, :], (V7X_SUBLANES, D_RNN))
        return carry

    pos0 = (t * TQ + lax.broadcasted_iota(jnp.int32, (TQ, GATE_CHUNK), 0)) == 0
    for c in range(N_GATE_CHUNKS):
        cols = slice(c * GATE_CHUNK, (c + 1) * GATE_CHUNK)
        u = zb_ref[:, OFF_U + c * GATE_CHUNK:OFF_U + (c + 1) * GATE_CHUNK]
        uext_ref[rd, 0:base, cols] = jnp.where(t > 0, uext_ref[rd, 0:base, cols], 0.0)
        uext_ref[rd, base:base + TQ, cols] = u
        cw = convw_ref[:, cols]
        cs = uext_ref[rd, base - 3:base - 3 + TQ, cols] * cw[0:1]
        cs = cs + uext_ref[rd, base - 2:base - 2 + TQ, cols] * cw[1:2]
        cs = cs + uext_ref[rd, base - 1:base - 1 + TQ, cols] * cw[2:3]
        cs = cs + u * cw[3:4]
        uc = prow(ROW_CONVB, cols) + cs
        if par == 0:
            convst_ref[0, :, cols] = uext_ref[rd, base + TQ - (CONV_WIDTH - 1):base + TQ, cols]
        uext_ref[wr, 0:base, cols] = u[TQ - base:TQ, :]
        fill(2)
        a, mult, i = _lru_gates(uc, wab_ref[c], prow(ROW_BA, cols), prow(ROW_BX, cols), prow(ROW_LAM, cols))
        a_ref[:, cols] = a
        b_ref[:, cols] = jnp.where(pos0, 1.0, mult) * i * uc
        fill(2)

    attn_chunks = [attention_head(0), attention_head(1)]
    n_groups = TQ // V7X_SUBLANES
    carry = scan_groups(jnp.where(t > 0, hcarry_ref[rd], 0.0), 0, n_groups // 2)
    attn_chunks.append(attention_head(2))
    carry = scan_groups(carry, n_groups // 2, n_groups)
    hcarry_ref[wr] = carry
    if par == 0:
        lru_ref[0] = carry[0:1, :]
    attn_chunks.append(attention_head(3))

    merged = []
    for c in range(D_MODEL // V7X_MXU_DIM):
        cols = slice(c * V7X_MXU_DIM, (c + 1) * V7X_MXU_DIM)
        zcol = lambda off: zb_ref[:, off + c * V7X_MXU_DIM:off + (c + 1) * V7X_MXU_DIM]
        rnn_out = jax.nn.gelu(zcol(OFF_G)) * h_ref[:, cols]
        merged.append((jax.nn.sigmoid(zcol(OFF_GA)) * attn_chunks[c]
                       + jax.nn.sigmoid(zcol(OFF_GR)) * rnn_out).astype(_BF16))
        fill(1)
    mix = _dot_col_tiles(jnp.concatenate(merged, axis=1), wout_ref)
    fill(3)
    x1 = x_res + _rmsnorm(mix, prow(ROW_GPOST))
    x1b_ref[...] = x1
    hnb_ref[...] = _rmsnorm(x1, prow(ROW_GPRE2)).astype(_BF16)
    fill(sum(w for w, _ in pieces) + len(pieces))
    assert not pieces


def _const_spec(shape):
    zeros = (0,) * len(shape)
    return pl.BlockSpec(shape, lambda *_: zeros, pipeline_mode=pl.Buffered(1))


def _layer_prompt(x, sinks, gpre, gpost, win_b, wout_b, convw, convb, wab, ba, bx, lam, gpre2, gpost2, wup_b, wdn_b):
    B, T, D = x.shape
    nt = T // TQ
    n_blocks = B * nt
    rows = jnp.stack([gpre, gpost, convb, ba, bx, lam, gpre2, gpost2])
    n_pairs = n_blocks // 2
    ntp = nt // 2
    cur = lambda s: jnp.minimum(s, n_pairs - 1)
    prv = lambda s: jnp.maximum(s - 1, 0)
    row_b = lambda s: jnp.maximum(2 * s - 1, 0) // nt
    return pl.pallas_call(
        functools.partial(_layer_prompt_kernel, nt=nt),
        grid=(n_pairs + 1,),
        in_specs=[
            pl.BlockSpec(memory_space=pltpu.SMEM),
            pl.BlockSpec((1, 2 * TQ, D), lambda s: (cur(s) // ntp, cur(s) % ntp, 0)),
            pl.BlockSpec((1, 2 * TQ, D), lambda s: (prv(s) // ntp, prv(s) % ntp, 0)),
            _const_spec((8, D)),
            _const_spec((D, D_IN)), _const_spec((D // V7X_MXU_DIM, D, V7X_MXU_DIM)),
            _const_spec((CONV_WIDTH, D_RNN)),
            _const_spec((N_GATE_CHUNKS, GATE_CHUNK, 2 * GATE_CHUNK)),
            _const_spec((D_FF // FFN_UP_COLS, D, FFN_UP_COLS)), _const_spec((D // V7X_MXU_DIM, D_FF, V7X_MXU_DIM)),
        ],
        out_specs=[
            pl.BlockSpec((1, 2 * TQ, D), lambda s: (prv(s) // ntp, prv(s) % ntp, 0)),
            pl.BlockSpec((1, WINDOW, D_KV), lambda s: (row_b(s), 0, 0)),
            pl.BlockSpec((1, WINDOW, D_KV), lambda s: (row_b(s), 0, 0)),
            pl.BlockSpec((1, CONV_WIDTH - 1, D_RNN), lambda s: (row_b(s), 0, 0)),
            pl.BlockSpec((1, 1, D_RNN), lambda s: (row_b(s), 0, 0)),
        ],
        out_shape=[
            jax.ShapeDtypeStruct((B, T, D), _F32),
            jax.ShapeDtypeStruct((B, WINDOW, D_KV), _F32),
            jax.ShapeDtypeStruct((B, WINDOW, D_KV), _F32),
            jax.ShapeDtypeStruct((B, CONV_WIDTH - 1, D_RNN), _F32),
            jax.ShapeDtypeStruct((B, 1, D_RNN), _F32),
        ],
        scratch_shapes=[
            pltpu.VMEM((TQ, D_IN), _F32), pltpu.VMEM((TQ, D_IN), _F32),
            pltpu.VMEM((TQ, D), _F32), pltpu.VMEM((TQ, D), _F32),
            pltpu.VMEM((TQ, D), _BF16), pltpu.VMEM((TQ, D), _BF16),
            pltpu.VMEM((2 * N_KV_HEADS, TQ, D_KV), _BF16),
            pltpu.VMEM((2 * N_KV_HEADS, HEAD_DIM, TQ), _BF16),
            pltpu.VMEM((2, V7X_SUBLANES + TQ, D_RNN), _F32),
            pltpu.VMEM((2, V7X_SUBLANES, D_RNN), _F32),
            *[pltpu.VMEM(shape, dtype) for shape, dtype in
              (((TQ, D_FF), _BF16), ((TQ, D), _F32), ((TQ, D_RNN), _F32), ((TQ, D_RNN), _F32), ((TQ, D_RNN), _F32))
              for _ in range(2)],
        ],
        compiler_params=pltpu.CompilerParams(
            dimension_semantics=("arbitrary",), vmem_limit_bytes=V7X_VMEM_LIMIT_BYTES),
        name="layer_prompt",
    )(sinks, x, x, rows, win_b, wout_b, convw, wab, wup_b, wdn_b)


def _mix_sample_kernel(x_ref, ck_ref, cv_ref, cst_ref, h0_ref, sinks_ref, gpre_ref, gpost_ref, win_ref, wout_ref,
                       convw_ref, convb_ref, wab_ref, ba_ref, bx_ref, lam_ref,
                       x1_ref, kwin_ref, vwin_ref, convst_ref, lru_ref,
                       qrow_ref, orow_ref, knew_ref, vnew_ref, rnn_ref, siga_ref):
    step = pl.program_id(0)
    last = pl.num_programs(0) - 1
    nb = x_ref.shape[0]
    blk = _lane_block((nb, D_KV))

    @pl.when(step == 0)
    def _():
        xnb = _rmsnorm(x_ref[...], gpre_ref[...]).astype(_BF16)
        qkv = _dot(xnb, win_ref[:, OFF_Q:OFF_U])
        knew_ref[...] = qkv[:, OFF_K:OFF_V]
        vnew_ref[...] = qkv[:, OFF_V:OFF_U]
        scale = HEAD_DIM ** -0.5
        for h in range(N_KV_HEADS):
            qc = qkv[:, h * D_KV:(h + 1) * D_KV] * scale
            for g in range(GROUP):
                shift = ((h - g) % GROUP) * HEAD_DIM
                moved = pltpu.roll(qc, shift, axis=1) if shift else qc
                hd = h * GROUP + g
                placed = jnp.where(blk == h, moved, 0.0)
                for half in range(D_KV // V7X_LANES):
                    qrow_ref[half, hd * nb:(hd + 1) * nb, :] = placed[:, half * V7X_LANES:(half + 1) * V7X_LANES]

        u = _dot(xnb, win_ref[:, OFF_U:OFF_G])
        cw = convw_ref[...]
        cs = cst_ref[0] * cw[0:1]
        cs = cs + cst_ref[1] * cw[1:2]
        cs = cs + cst_ref[2] * cw[2:3]
        cs = cs + u * cw[3:4]
        uc = convb_ref[...] + cs
        convst_ref[0] = cst_ref[1]
        convst_ref[1] = cst_ref[2]
        convst_ref[2] = u
        parts = [_lru_gates(uc[:, c * GATE_CHUNK:(c + 1) * GATE_CHUNK], wab_ref[c],
                            *(r[:, c * GATE_CHUNK:(c + 1) * GATE_CHUNK] for r in (ba_ref, bx_ref, lam_ref)))
                 for c in range(N_GATE_CHUNKS)]
        a, mult, i = (jnp.concatenate(p, axis=1) for p in zip(*parts))
        hnew = a * h0_ref[...] + mult * i * uc
        lru_ref[...] = hnew
        g_branch = _dot(xnb, win_ref[:, OFF_G:OFF_GA])
        gate_r = _dot(xnb, win_ref[:, OFF_GR:D_IN])
        rnn_ref[...] = jax.nn.sigmoid(gate_r) * (jax.nn.gelu(g_branch) * hnew)
        siga_ref[...] = jax.nn.sigmoid(_dot(xnb, win_ref[:, OFF_GA:OFF_GR]))

    sink = sinks_ref[...]
    rowid = lax.broadcasted_iota(jnp.int32, (WINDOW, D_KV), 0)

    def per_row(bl, carry):
        b = step * SAMPLE_BC + bl
        kb = ck_ref[bl]
        vb = cv_ref[bl]
        kn = knew_ref[pl.ds(b, 1), :]
        vn = vnew_ref[pl.ds(b, 1), :]
        kwin_ref[bl] = jnp.where(rowid == WINDOW - 1, kn, pltpu.roll(kb, WINDOW - 1, axis=0))
        vwin_ref[bl] = jnp.where(rowid == WINDOW - 1, vn, pltpu.roll(vb, WINDOW - 1, axis=0))

        head_rows = pl.ds(b, N_HEADS, stride=nb)
        qm = jnp.concatenate([qrow_ref[half, head_rows, :] for half in range(D_KV // V7X_LANES)],
                             axis=1).astype(_BF16)
        s = _dot_nt(qm, kb.astype(_BF16))
        knr = kn.astype(_BF16).astype(_F32)
        s_self = jnp.sum(qm.astype(_F32) * knr, axis=-1, keepdims=True)
        m = jnp.maximum(jnp.maximum(jnp.max(s, axis=-1, keepdims=True), s_self), sink)
        p = jnp.exp(s - m)
        p_self = jnp.exp(s_self - m)
        den = jnp.sum(p, axis=-1, keepdims=True) + p_self + jnp.exp(sink - m)
        o = _dot(p.astype(_BF16), vb.astype(_BF16))
        o = o + p_self.astype(_BF16).astype(_F32) * vn.astype(_BF16).astype(_F32)
        o = o * (1.0 / den)
        for half in range(D_KV // V7X_LANES):
            orow_ref[half, head_rows, :] = o[:, half * V7X_LANES:(half + 1) * V7X_LANES]
        return carry

    lax.fori_loop(0, SAMPLE_BC, per_row, 0, unroll=True)

    @pl.when(step == last)
    def _():
        chunks = []
        for h in range(N_KV_HEADS):
            acc = None
            for g in range(GROUP):
                hd = h * GROUP + g
                rows = jnp.concatenate([orow_ref[half, hd * nb:(hd + 1) * nb, :]
                                        for half in range(D_KV // V7X_LANES)], axis=1)
                sel = jnp.where(blk == h, rows, 0.0)
                shift = ((g - h) % GROUP) * HEAD_DIM
                moved = pltpu.roll(sel, shift, axis=1) if shift else sel
                acc = moved if acc is None else acc + moved
            chunks.append(acc)
        attn = jnp.concatenate(chunks, axis=1)
        merged = (siga_ref[...] * attn + rnn_ref[...]).astype(_BF16)
        mix = _dot_col_tiles(merged, wout_ref)
        x1_ref[...] = x_ref[...] + _rmsnorm(mix, gpost_ref[...])


def _mix_sample(x, ck, cv, cst, h0, sinks, gpre, gpost, win_b, wout_b, convw, convb, wab, ba, bx, lam):
    nb, D = x.shape
    row = lambda a: a.reshape(1, -1)
    cache_spec = pl.BlockSpec((SAMPLE_BC, WINDOW, D_KV), lambda s: (s, 0, 0))
    return pl.pallas_call(
        _mix_sample_kernel,
        grid=(nb // SAMPLE_BC,),
        in_specs=[
            _const_spec((nb, D)), cache_spec, cache_spec,
            _const_spec((CONV_WIDTH - 1, nb, D_RNN)), _const_spec((nb, D_RNN)),
            _const_spec((N_HEADS, 1)),
            _const_spec((1, D)), _const_spec((1, D)),
            _const_spec((D, D_IN)), _const_spec((D // V7X_MXU_DIM, D, V7X_MXU_DIM)),
            _const_spec((CONV_WIDTH, D_RNN)), _const_spec((1, D_RNN)),
            _const_spec((N_GATE_CHUNKS, GATE_CHUNK, 2 * GATE_CHUNK)),
            _const_spec((1, D_RNN)), _const_spec((1, D_RNN)), _const_spec((1, D_RNN)),
        ],
        out_specs=[
            pl.BlockSpec((nb, D), lambda s: (0, 0)),
            cache_spec, cache_spec,
            pl.BlockSpec((CONV_WIDTH - 1, nb, D_RNN), lambda s: (0, 0, 0)),
            pl.BlockSpec((nb, D_RNN), lambda s: (0, 0)),
        ],
        out_shape=[
            jax.ShapeDtypeStruct((nb, D), _F32),
            jax.ShapeDtypeStruct((nb, WINDOW, D_KV), _F32),
            jax.ShapeDtypeStruct((nb, WINDOW, D_KV), _F32),
            jax.ShapeDtypeStruct((CONV_WIDTH - 1, nb, D_RNN), _F32),
            jax.ShapeDtypeStruct((nb, D_RNN), _F32),
        ],
        scratch_shapes=[
            pltpu.VMEM((D_KV // V7X_LANES, N_HEADS * nb, V7X_LANES), _F32),
            pltpu.VMEM((D_KV // V7X_LANES, N_HEADS * nb, V7X_LANES), _F32),
            pltpu.VMEM((nb, D_KV), _F32), pltpu.VMEM((nb, D_KV), _F32),
            pltpu.VMEM((nb, D_RNN), _F32), pltpu.VMEM((nb, D), _F32),
        ],
        compiler_params=pltpu.CompilerParams(
            dimension_semantics=("arbitrary",), vmem_limit_bytes=V7X_VMEM_LIMIT_BYTES),
        name="mix_sample",
    )(x, ck, cv, cst, h0, sinks.reshape(N_HEADS, 1), row(gpre), row(gpost), win_b, wout_b, convw, row(convb),
      wab, row(ba), row(bx), row(lam))


def _ffn_kernel(x_ref, gpre_ref, gpost_ref, wup_ref, wdn_ref, o_ref):
    x = x_ref[...]
    hn = _rmsnorm(x, gpre_ref[...]).astype(_BF16)
    up = _dot_col_tiles(hn, wup_ref)
    act = jnp.square(jnp.maximum(up, 0.0)).astype(_BF16)
    f = _dot_col_tiles(act, wdn_ref)
    o_ref[...] = x + _rmsnorm(f, gpost_ref[...])


def _ffn(x, gpre, gpost, wup_b, wdn_b, tm):
    M, D = x.shape
    row = lambda a: a.reshape(1, -1)
    return pl.pallas_call(
        _ffn_kernel,
        grid=(M // tm,),
        in_specs=[
            pl.BlockSpec((tm, D), lambda i: (i, 0)),
            _const_spec((1, D)), _const_spec((1, D)),
            _const_spec((D_FF // FFN_UP_COLS, D, FFN_UP_COLS)), _const_spec((D // V7X_MXU_DIM, D_FF, V7X_MXU_DIM)),
        ],
        out_specs=pl.BlockSpec((tm, D), lambda i: (i, 0)),
        out_shape=jax.ShapeDtypeStruct((M, D), _F32),
        compiler_params=pltpu.CompilerParams(
            dimension_semantics=("arbitrary",), vmem_limit_bytes=V7X_VMEM_LIMIT_BYTES),
        name="ffn",
    )(x, row(gpre), row(gpost), wup_b, wdn_b)


def _gate_weights(w_a, w_x):
    per = GATE_CHUNK // RNN_BLOCK
    eye = jnp.eye(per, dtype=bool)[None, :, None, :, None]

    def bd(w):
        w = w.reshape(N_GATE_CHUNKS, per, RNN_BLOCK, 1, RNN_BLOCK)
        return jnp.where(eye, w, 0.0).reshape(N_GATE_CHUNKS, GATE_CHUNK, GATE_CHUNK)

    return jnp.concatenate([bd(w_a), bd(w_x)], axis=-1).astype(_BF16)


def kernel(x_prompt, x_sample, cache_k_win, cache_v_win, state_conv, state_lru, w_in, w_out, sinks, conv_w, conv_b,
           lru_w_a, lru_b_a, lru_w_x, lru_b_x, lru_lambda, w_up, w_down, g_pre_mix, g_post_mix, g_pre_ffn,
           g_post_ffn):
    B, T, D = x_prompt.shape
    DB, DT, _ = x_sample.shape
    depth = w_in.shape[0]
    assert depth == 1 and DT == 1 and T % (2 * TQ) == 0 and DB % SAMPLE_BC == 0
    l = 0
    win_b = w_in[l].astype(_BF16)
    wout_b = _col_tiles(w_out[l], V7X_MXU_DIM)
    wup_b = _col_tiles(w_up[l], FFN_UP_COLS)
    wdn_b = _col_tiles(w_down[l], V7X_MXU_DIM)
    wab = _gate_weights(lru_w_a[l], lru_w_x[l])
    layer = (g_pre_mix[l], g_post_mix[l], win_b, wout_b, conv_w[l], conv_b[l], wab, lru_b_a[l], lru_b_x[l],
             lru_lambda[l])

    yp, kp, vp, cp, hp = _layer_prompt(x_prompt, sinks[l], *layer, g_pre_ffn[l], g_post_ffn[l], wup_b, wdn_b)

    x1s, kq, vq, cq, hq = _mix_sample(
        x_sample.reshape(DB, D), cache_k_win[l].reshape(DB, WINDOW, D_KV), cache_v_win[l].reshape(DB, WINDOW, D_KV),
        jnp.swapaxes(state_conv[l], 0, 1), state_lru[l], sinks[l], *layer)
    ys = _ffn(x1s, g_pre_ffn[l], g_post_ffn[l], wup_b, wdn_b, DB).reshape(DB, DT, D)

    kv_shape = (1, -1, WINDOW, N_KV_HEADS, HEAD_DIM)
    return (yp, ys, kp.reshape(kv_shape), vp.reshape(kv_shape), cp[None], hp.reshape(1, B, D_RNN),
            kq.reshape(kv_shape), vq.reshape(kv_shape), jnp.swapaxes(cq, 0, 1)[None], hq[None])
```

```python
import functools

import jax
import jax.numpy as jnp
from jax import lax
from jax.experimental import pallas as pl
from jax.experimental.pallas import tpu as pltpu

D_MODEL = 1024
N_HEADS = 16
N_KV_HEADS = 4
HEAD_DIM = 64
GROUP = N_HEADS // N_KV_HEADS
D_ATTN = N_HEADS * HEAD_DIM
D_KV = N_KV_HEADS * HEAD_DIM
WINDOW = 128
D_RNN = D_MODEL
N_RNN_BLOCKS = 16
RNN_BLOCK = D_RNN // N_RNN_BLOCKS
CONV_WIDTH = 4
LRU_C = 8.0
D_FF = 4 * D_MODEL
EPS = 1e-6

OFF_Q = 0
OFF_K = D_ATTN
OFF_V = D_ATTN + D_KV
OFF_U = D_ATTN + 2 * D_KV
OFF_G = OFF_U + D_RNN
OFF_GA = OFF_G + D_RNN
OFF_GR = OFF_GA + D_MODEL
D_IN = OFF_GR + D_MODEL

V7X_SUBLANES = 8
V7X_LANES = 128
V7X_MXU_DIM = 256
V7X_VMEM_LIMIT_BYTES = 60 * 1024 * 1024

TQ = WINDOW
GATE_CHUNK = V7X_MXU_DIM
N_GATE_CHUNKS = D_RNN // GATE_CHUNK
SAMPLE_BC = 16
FFN_UP_COLS = 512

_BF16 = jnp.bfloat16
_F32 = jnp.float32


def _dot(a, b):
    return jnp.dot(a, b, preferred_element_type=_F32)


def _dot_nt(a, b):
    return lax.dot_general(a, b, (((1,), (1,)), ((), ())), preferred_element_type=_F32)


def _dot_col_tiles(a, w_ref):
    return jnp.concatenate([_dot(a, w_ref[j]) for j in range(w_ref.shape[0])], axis=1)


def _col_tiles(w, cols):
    k, n = w.shape
    return jnp.transpose(w.astype(_BF16).reshape(k, n // cols, cols), (1, 0, 2))


def _rmsnorm(x, g):
    return x * lax.rsqrt(jnp.mean(x * x, axis=-1, keepdims=True) + EPS) * g


def _softplus(y):
    return jnp.maximum(y, 0.0) + jnp.log1p(jnp.exp(-jnp.abs(y)))


def _lane_block(shape):
    return lax.broadcasted_iota(jnp.int32, shape, len(shape) - 1) // HEAD_DIM


def _lru_gates(uc, wab, ba, bx, lam):
    gc = _dot(uc.astype(_BF16), wab)
    r = jax.nn.sigmoid(gc[:, :GATE_CHUNK] + ba)
    i = jax.nn.sigmoid(gc[:, GATE_CHUNK:] + bx)
    log_a = -LRU_C * r * _softplus(-lam)
    a = jnp.exp(log_a)
    m2 = -jnp.tanh(log_a) * (a * a + 1.0)
    mult = jnp.where(m2 > 0.0, m2 * lax.rsqrt(m2), 0.0)
    return a, mult, i


def _replicate_head_blocks(m):
    lo, hi = m[:, :V7X_LANES], m[:, V7X_LANES:]
    rlo = pltpu.roll(lo, HEAD_DIM, axis=1)
    rhi = pltpu.roll(hi, HEAD_DIM, axis=1)
    first = lax.broadcasted_iota(jnp.int32, lo.shape, 1) < HEAD_DIM
    halves = (jnp.where(first, lo, rlo), jnp.where(first, rlo, lo),
              jnp.where(first, hi, rhi), jnp.where(first, rhi, hi))
    return [jnp.concatenate([b, b], axis=1) for b in halves]


N_SCRATCH = 20
ROW_GPRE, ROW_GPOST, ROW_CONVB, ROW_BA, ROW_BX, ROW_LAM, ROW_GPRE2, ROW_GPOST2 = range(8)


def _layer_prompt_kernel(*refs, nt):
    step = pl.program_id(0)
    scratch = refs[-N_SCRATCH:]

    @pl.when(step == 0)
    def _():
        for ref in scratch:
            ref[...] = jnp.zeros(ref.shape, ref.dtype)

    for par in (0, 1):
        _layer_prompt_step(par, step, *refs, nt=nt)


def _layer_prompt_step(par, step, sinks_ref, xa_ref, xb_ref, rows_ref, win_ref, wout_ref, convw_ref,
                       wab_ref, wup_ref, wdn_ref,
                       y_ref, kwin_ref, vwin_ref, convst_ref, lru_ref,
                       z0_ref, z1_ref, x10_ref, x11_ref, hn0_ref, hn1_ref, kprev_ref, vprev_ref, uext_ref, hcarry_ref,
                       *temps, nt):
    act_ref, f_ref, a_ref, b_ref, h_ref = temps[par::2]
    za_ref, zb_ref = (z0_ref, z1_ref) if par == 0 else (z1_ref, z0_ref)
    x1c_ref, x1b_ref = (x10_ref, x11_ref) if par == 0 else (x11_ref, x10_ref)
    hnc_ref, hnb_ref = (hn0_ref, hn1_ref) if par == 0 else (hn1_ref, hn0_ref)
    rows_a = slice(par * TQ, (par + 1) * TQ)
    x_res = xb_ref[0, TQ:2 * TQ, :] if par == 0 else xa_ref[0, 0:TQ, :]
    rows_y = slice(par * TQ, (par + 1) * TQ)

    prow = lambda i, cols=slice(None): rows_ref[i:i + 1, cols]
    xnb = _rmsnorm(xa_ref[0, rows_a, :], prow(ROW_GPRE)).astype(_BF16)

    def a_tile(j):
        c = j * V7X_MXU_DIM
        za_ref[:, c:c + V7X_MXU_DIM] = _dot(xnb, win_ref[:, c:c + V7X_MXU_DIM])

    def c_up(j):
        c = j * FFN_UP_COLS
        up = _dot(hnc_ref[...], wup_ref[j])
        act_ref[:, c:c + FFN_UP_COLS] = jnp.square(jnp.maximum(up, 0.0)).astype(_BF16)

    def c_down(j):
        c = j * V7X_MXU_DIM
        f_ref[:, c:c + V7X_MXU_DIM] = _dot(act_ref[...], wdn_ref[j])

    def c_tail():
        y_ref[0, rows_y, :] = x1c_ref[...] + _rmsnorm(f_ref[...], prow(ROW_GPOST2))

    n_a, n_up, n_dn = D_IN // V7X_MXU_DIM, D_FF // FFN_UP_COLS, D_MODEL // V7X_MXU_DIM
    w_up, w_dn = FFN_UP_COLS // V7X_MXU_DIM, D_FF // D_MODEL
    a_iter = iter([(1, functools.partial(a_tile, j)) for j in range(n_a)])
    pieces = []
    for j in range(n_up):
        pieces.append((w_up, functools.partial(c_up, j)))
        pieces.extend(p for _, p in zip(range(1), a_iter))
    for j in range(n_dn):
        pieces.append((w_dn, functools.partial(c_down, j)))
        pieces.extend(p for _, p in zip(range(1), a_iter))
    pieces.append((0, c_tail))
    pieces.extend(a_iter)
    pieces.reverse()
    credit = [0.0]

    def fill(units):
        credit[0] += units
        while pieces and credit[0] >= 0.5 * pieces[-1][0]:
            weight, emit = pieces.pop()
            emit()
            credit[0] -= weight

    fill(2)

    t = (2 * step + par + nt - 1) % nt
    assert nt % 2 == 0
    rd, wr = 1 - par, par
    base = V7X_SUBLANES

    k = zb_ref[:, OFF_K:OFF_V]
    v = zb_ref[:, OFF_V:OFF_U]
    if par == 0:
        kwin_ref[0] = k
        vwin_ref[0] = v
    kcur = [a.astype(_BF16) for a in _replicate_head_blocks(k)]
    vt = jnp.transpose(v).astype(_BF16)
    vcur = [vt[h * HEAD_DIM:(h + 1) * HEAD_DIM, :] for h in range(N_KV_HEADS)]
    for h in range(N_KV_HEADS):
        kprev_ref[wr * N_KV_HEADS + h] = kcur[h]
        vprev_ref[wr * N_KV_HEADS + h] = vcur[h]

    kj = lax.broadcasted_iota(jnp.int32, (2 * TQ, TQ), 0)
    qi = lax.broadcasted_iota(jnp.int32, (2 * TQ, TQ), 1)
    diff = qi + TQ - kj
    first_key = jnp.where(t > 0, 0, TQ)
    mask = (diff >= 0) & (diff <= WINDOW) & (kj >= first_key)
    blk = _lane_block((TQ, D_KV))
    scale = HEAD_DIM ** -0.5

    def attention_head(h):
        qc = zb_ref[:, h * D_KV:(h + 1) * D_KV] * scale
        qm = jnp.concatenate([jnp.where(blk == g, qc, 0.0).astype(_BF16) for g in range(GROUP)], axis=0)
        kk = jnp.concatenate([kprev_ref[rd * N_KV_HEADS + h], kcur[h]], axis=0)
        vv = jnp.concatenate([vprev_ref[rd * N_KV_HEADS + h], vcur[h]], axis=1)
        sc = _dot_nt(kk, qm)
        ps, invs = [], []
        for g in range(GROUP):
            sg = jnp.where(mask, sc[:, g * TQ:(g + 1) * TQ], -jnp.inf)
            sink = sinks_ref[h * GROUP + g]
            m = jnp.maximum(jnp.max(sg, axis=0, keepdims=True), sink)
            p = jnp.exp(sg - m)
            den = jnp.sum(p, axis=0, keepdims=True) + jnp.exp(sink - m)
            invs.append(1.0 / den)
            ps.append(p.astype(_BF16))
            if g % 2 == 1:
                fill(1)
        o = _dot(vv, jnp.concatenate(ps, axis=1))
        o = jnp.concatenate([o[:, g * TQ:(g + 1) * TQ] * invs[g] for g in range(GROUP)], axis=0)
        fill(1)
        return jnp.transpose(o)

    def scan_groups(carry, first, last):
        sub = lax.broadcasted_iota(jnp.int32, (V7X_SUBLANES, D_RNN), 0)
        for j in range(first, last):
            rows = slice(j * V7X_SUBLANES, (j + 1) * V7X_SUBLANES)
            aj = a_ref[rows, :]
            bj = b_ref[rows, :]
            for sft in (1, 2, 4):
                keep = sub >= sft
                bj = jnp.where(keep, aj * pltpu.roll(bj, sft, axis=0) + bj, bj)
                aj = jnp.where(keep, aj * pltpu.roll(aj, sft, axis=0), aj)
            hj = aj * carry + bj
            h_ref[rows, :] = hj
            if j % 2 == 1:
                fill(2 if j % 4 == 1 else 1)
            carry = jnp.broadcast_to(hj[V7X_SUBLANES - 1:V7X_SUBLANES, :], (V7X_SUBLANES, D_RNN))
        return carry

    pos0 = (t * TQ + lax.broadcasted_iota(jnp.int32, (TQ, GATE_CHUNK), 0)) == 0
    for c in range(N_GATE_CHUNKS):
        cols = slice(c * GATE_CHUNK, (c + 1) * GATE_CHUNK)
        u = zb_ref[:, OFF_U + c * GATE_CHUNK:OFF_U + (c + 1) * GATE_CHUNK]
        uext_ref[rd, 0:base, cols] = jnp.where(t > 0, uext_ref[rd, 0:base, cols], 0.0)
        uext_ref[rd, base:base + TQ, cols] = u
        cw = convw_ref[:, cols]
        cs = uext_ref[rd, base - 3:base - 3 + TQ, cols] * cw[0:1]
        cs = cs + uext_ref[rd, base - 2:base - 2 + TQ, cols] * cw[1:2]
        cs = cs + uext_ref[rd, base - 1:base - 1 + TQ, cols] * cw[2:3]
        cs = cs + u * cw[3:4]
        uc = prow(ROW_CONVB, cols) + cs
        if par == 0:
            convst_ref[0, :, cols] = uext_ref[rd, base + TQ - (CONV_WIDTH - 1):base + TQ, cols]
        uext_ref[wr, 0:base, cols] = u[TQ - base:TQ, :]
        fill(2)
        a, mult, i = _lru_gates(uc, wab_ref[c], prow(ROW_BA, cols), prow(ROW_BX, cols), prow(ROW_LAM, cols))
        a_ref[:, cols] = a
        b_ref[:, cols] = jnp.where(pos0, 1.0, mult) * i * uc
        fill(2)

    attn_chunks = [attention_head(0), attention_head(1)]
    n_groups = TQ // V7X_SUBLANES
    carry = scan_groups(jnp.where(t > 0, hcarry_ref[rd], 0.0), 0, n_groups // 2)
    attn_chunks.append(attention_head(2))
    carry = scan_groups(carry, n_groups // 2, n_groups)
    hcarry_ref[wr] = carry
    if par == 0:
        lru_ref[0] = carry[0:1, :]
    attn_chunks.append(attention_head(3))

    merged = []
    for c in range(D_MODEL // V7X_MXU_DIM):
        cols = slice(c * V7X_MXU_DIM, (c + 1) * V7X_MXU_DIM)
        zcol = lambda off: zb_ref[:, off + c * V7X_MXU_DIM:off + (c + 1) * V7X_MXU_DIM]
        rnn_out = jax.nn.gelu(zcol(OFF_G)) * h_ref[:, cols]
        merged.append((jax.nn.sigmoid(zcol(OFF_GA)) * attn_chunks[c]
                       + jax.nn.sigmoid(zcol(OFF_GR)) * rnn_out).astype(_BF16))
        fill(1)
    mix = _dot_col_tiles(jnp.concatenate(merged, axis=1), wout_ref)
    fill(3)
    x1 = x_res + _rmsnorm(mix, prow(ROW_GPOST))
    x1b_ref[...] = x1
    hnb_ref[...] = _rmsnorm(x1, prow(ROW_GPRE2)).astype(_BF16)
    fill(sum(w for w, _ in pieces) + len(pieces))
    assert not pieces


def _const_spec(shape):
    zeros = (0,) * len(shape)
    return pl.BlockSpec(shape, lambda *_: zeros, pipeline_mode=pl.Buffered(1))


def _layer_prompt(x, sinks, gpre, gpost, win_b, wout_b, convw, convb, wab, ba, bx, lam, gpre2, gpost2, wup_b, wdn_b):
    B, T, D = x.shape
    nt = T // TQ
    n_blocks = B * nt
    rows = jnp.stack([gpre, gpost, convb, ba, bx, lam, gpre2, gpost2])
    n_pairs = n_blocks // 2
    ntp = nt // 2
    cur = lambda s: jnp.minimum(s, n_pairs - 1)
    prv = lambda s: jnp.maximum(s - 1, 0)
    row_b = lambda s: jnp.maximum(2 * s - 1, 0) // nt
    return pl.pallas_call(
        functools.partial(_layer_prompt_kernel, nt=nt),
        grid=(n_pairs + 1,),
        in_specs=[
            pl.BlockSpec(memory_space=pltpu.SMEM),
            pl.BlockSpec((1, 2 * TQ, D), lambda s: (cur(s) // ntp, cur(s) % ntp, 0)),
            pl.BlockSpec((1, 2 * TQ, D), lambda s: (prv(s) // ntp, prv(s) % ntp, 0)),
            _const_spec((8, D)),
            _const_spec((D, D_IN)), _const_spec((D // V7X_MXU_DIM, D, V7X_MXU_DIM)),
            _const_spec((CONV_WIDTH, D_RNN)),
            _const_spec((N_GATE_CHUNKS, GATE_CHUNK, 2 * GATE_CHUNK)),
            _const_spec((D_FF // FFN_UP_COLS, D, FFN_UP_COLS)), _const_spec((D // V7X_MXU_DIM, D_FF, V7X_MXU_DIM)),
        ],
        out_specs=[
            pl.BlockSpec((1, 2 * TQ, D), lambda s: (prv(s) // ntp, prv(s) % ntp, 0)),
            pl.BlockSpec((1, WINDOW, D_KV), lambda s: (row_b(s), 0, 0)),
            pl.BlockSpec((1, WINDOW, D_KV), lambda s: (row_b(s), 0, 0)),
            pl.BlockSpec((1, CONV_WIDTH - 1, D_RNN), lambda s: (row_b(s), 0, 0)),
            pl.BlockSpec((1, 1, D_RNN), lambda s: (row_b(s), 0, 0)),
        ],
        out_shape=[
            jax.ShapeDtypeStruct((B, T, D), _F32),
            jax.ShapeDtypeStruct((B, WINDOW, D_KV), _F32),
            jax.ShapeDtypeStruct((B, WINDOW, D_KV), _F32),
            jax.ShapeDtypeStruct((B, CONV_WIDTH - 1, D_RNN), _F32),
            jax.ShapeDtypeStruct((B, 1, D_RNN), _F32),
        ],
        scratch_shapes=[
            pltpu.VMEM((TQ, D_IN), _F32), pltpu.VMEM((TQ, D_IN), _F32),
            pltpu.VMEM((TQ, D), _F32), pltpu.VMEM((TQ, D), _F32),
            pltpu.VMEM((TQ, D), _BF16), pltpu.VMEM((TQ, D), _BF16),
            pltpu.VMEM((2 * N_KV_HEADS, TQ, D_KV), _BF16),
            pltpu.VMEM((2 * N_KV_HEADS, HEAD_DIM, TQ), _BF16),
            pltpu.VMEM((2, V7X_SUBLANES + TQ, D_RNN), _F32),
            pltpu.VMEM((2, V7X_SUBLANES, D_RNN), _F32),
            *[pltpu.VMEM(shape, dtype) for shape, dtype in
              (((TQ, D_FF), _BF16), ((TQ, D), _F32), ((TQ, D_RNN), _F32), ((TQ, D_RNN), _F32), ((TQ, D_RNN), _F32))
              for _ in range(2)],
        ],
        compiler_params=pltpu.CompilerParams(
            dimension_semantics=("arbitrary",), vmem_limit_bytes=V7X_VMEM_LIMIT_BYTES),
        name="layer_prompt",
    )(sinks, x, x, rows, win_b, wout_b, convw, wab, wup_b, wdn_b)


def _layer_sample_kernel(x_ref, ck_ref, cv_ref, cst_ref, h0_ref, sinks_ref, gpre_ref, gpost_ref, win_ref, wout_ref,
                         convw_ref, convb_ref, wab_ref, ba_ref, bx_ref, lam_ref, gpre2_ref, gpost2_ref, wup_ref, wdn_ref,
                         y_ref, kwin_ref, vwin_ref, convst_ref, lru_ref,
                       qrow_ref, orow_ref, knew_ref, vnew_ref, rnn_ref, siga_ref):
    step = pl.program_id(0)
    last = pl.num_programs(0) - 1
    nb = x_ref.shape[0]
    blk = _lane_block((nb, D_KV))

    @pl.when(step == 0)
    def _():
        xnb = _rmsnorm(x_ref[...], gpre_ref[...]).astype(_BF16)
        qkv = _dot(xnb, win_ref[:, OFF_Q:OFF_U])
        knew_ref[...] = qkv[:, OFF_K:OFF_V]
        vnew_ref[...] = qkv[:, OFF_V:OFF_U]
        scale = HEAD_DIM ** -0.5
        for h in range(N_KV_HEADS):
            qc = qkv[:, h * D_KV:(h + 1) * D_KV] * scale
            for g in range(GROUP):
                shift = ((h - g) % GROUP) * HEAD_DIM
                moved = pltpu.roll(qc, shift, axis=1) if shift else qc
                hd = h * GROUP + g
                placed = jnp.where(blk == h, moved, 0.0)
                for half in range(D_KV // V7X_LANES):
                    qrow_ref[half, hd * nb:(hd + 1) * nb, :] = placed[:, half * V7X_LANES:(half + 1) * V7X_LANES]

        u = _dot(xnb, win_ref[:, OFF_U:OFF_G])
        cw = convw_ref[...]
        cs = cst_ref[0] * cw[0:1]
        cs = cs + cst_ref[1] * cw[1:2]
        cs = cs + cst_ref[2] * cw[2:3]
        cs = cs + u * cw[3:4]
        uc = convb_ref[...] + cs
        convst_ref[0] = cst_ref[1]
        convst_ref[1] = cst_ref[2]
        convst_ref[2] = u
        parts = [_lru_gates(uc[:, c * GATE_CHUNK:(c + 1) * GATE_CHUNK], wab_ref[c],
                            *(r[:, c * GATE_CHUNK:(c + 1) * GATE_CHUNK] for r in (ba_ref, bx_ref, lam_ref)))
                 for c in range(N_GATE_CHUNKS)]
        a, mult, i = (jnp.concatenate(p, axis=1) for p in zip(*parts))
        hnew = a * h0_ref[...] + mult * i * uc
        lru_ref[...] = hnew
        g_branch = _dot(xnb, win_ref[:, OFF_G:OFF_GA])
        gate_r = _dot(xnb, win_ref[:, OFF_GR:D_IN])
        rnn_ref[...] = jax.nn.sigmoid(gate_r) * (jax.nn.gelu(g_branch) * hnew)
        siga_ref[...] = jax.nn.sigmoid(_dot(xnb, win_ref[:, OFF_GA:OFF_GR]))

    sink = sinks_ref[...]
    rowid = lax.broadcasted_iota(jnp.int32, (WINDOW, D_KV), 0)

    def per_row(bl, carry):
        b = step * SAMPLE_BC + bl
        kb = ck_ref[bl]
        vb = cv_ref[bl]
        kn = knew_ref[pl.ds(b, 1), :]
        vn = vnew_ref[pl.ds(b, 1), :]
        kwin_ref[bl] = jnp.where(rowid == WINDOW - 1, kn, pltpu.roll(kb, WINDOW - 1, axis=0))
        vwin_ref[bl] = jnp.where(rowid == WINDOW - 1, vn, pltpu.roll(vb, WINDOW - 1, axis=0))

        head_rows = pl.ds(b, N_HEADS, stride=nb)
        qm = jnp.concatenate([qrow_ref[half, head_rows, :] for half in range(D_KV // V7X_LANES)],
                             axis=1).astype(_BF16)
        s = _dot_nt(qm, kb.astype(_BF16))
        knr = kn.astype(_BF16).astype(_F32)
        s_self = jnp.sum(qm.astype(_F32) * knr, axis=-1, keepdims=True)
        m = jnp.maximum(jnp.maximum(jnp.max(s, axis=-1, keepdims=True), s_self), sink)
        p = jnp.exp(s - m)
        p_self = jnp.exp(s_self - m)
        den = jnp.sum(p, axis=-1, keepdims=True) + p_self + jnp.exp(sink - m)
        o = _dot(p.astype(_BF16), vb.astype(_BF16))
        o = o + p_self.astype(_BF16).astype(_F32) * vn.astype(_BF16).astype(_F32)
        o = o * (1.0 / den)
        for half in range(D_KV // V7X_LANES):
            orow_ref[half, head_rows, :] = o[:, half * V7X_LANES:(half + 1) * V7X_LANES]
        return carry

    lax.fori_loop(0, SAMPLE_BC, per_row, 0, unroll=True)

    @pl.when(step == last)
    def _():
        chunks = []
        for h in range(N_KV_HEADS):
            acc = None
            for g in range(GROUP):
                hd = h * GROUP + g
                rows = jnp.concatenate([orow_ref[half, hd * nb:(hd + 1) * nb, :]
                                        for half in range(D_KV // V7X_LANES)], axis=1)
                sel = jnp.where(blk == h, rows, 0.0)
                shift = ((g - h) % GROUP) * HEAD_DIM
                moved = pltpu.roll(sel, shift, axis=1) if shift else sel
                acc = moved if acc is None else acc + moved
            chunks.append(acc)
        attn = jnp.concatenate(chunks, axis=1)
        merged = (siga_ref[...] * attn + rnn_ref[...]).astype(_BF16)
        mix = _dot_col_tiles(merged, wout_ref)
        x1 = x_ref[...] + _rmsnorm(mix, gpost_ref[...])
        hn = _rmsnorm(x1, gpre2_ref[...]).astype(_BF16)
        act = jnp.square(jnp.maximum(_dot_col_tiles(hn, wup_ref), 0.0)).astype(_BF16)
        y_ref[...] = x1 + _rmsnorm(_dot_col_tiles(act, wdn_ref), gpost2_ref[...])


def _layer_sample(x, ck, cv, cst, h0, sinks, gpre, gpost, win_b, wout_b, convw, convb, wab, ba, bx, lam,
                  gpre2, gpost2, wup_b, wdn_b):
    nb, D = x.shape
    row = lambda a: a.reshape(1, -1)
    cache_spec = pl.BlockSpec((SAMPLE_BC, WINDOW, D_KV), lambda s: (s, 0, 0))
    return pl.pallas_call(
        _layer_sample_kernel,
        grid=(nb // SAMPLE_BC,),
        in_specs=[
            _const_spec((nb, D)), cache_spec, cache_spec,
            _const_spec((CONV_WIDTH - 1, nb, D_RNN)), _const_spec((nb, D_RNN)),
            _const_spec((N_HEADS, 1)),
            _const_spec((1, D)), _const_spec((1, D)),
            _const_spec((D, D_IN)), _const_spec((D // V7X_MXU_DIM, D, V7X_MXU_DIM)),
            _const_spec((CONV_WIDTH, D_RNN)), _const_spec((1, D_RNN)),
            _const_spec((N_GATE_CHUNKS, GATE_CHUNK, 2 * GATE_CHUNK)),
            _const_spec((1, D_RNN)), _const_spec((1, D_RNN)), _const_spec((1, D_RNN)),
            _const_spec((1, D)), _const_spec((1, D)),
            _const_spec((D_FF // FFN_UP_COLS, D, FFN_UP_COLS)), _const_spec((D // V7X_MXU_DIM, D_FF, V7X_MXU_DIM)),
        ],
        out_specs=[
            pl.BlockSpec((nb, D), lambda s: (0, 0)),
            cache_spec, cache_spec,
            pl.BlockSpec((CONV_WIDTH - 1, nb, D_RNN), lambda s: (0, 0, 0)),
            pl.BlockSpec((nb, D_RNN), lambda s: (0, 0)),
        ],
        out_shape=[
            jax.ShapeDtypeStruct((nb, D), _F32),
            jax.ShapeDtypeStruct((nb, WINDOW, D_KV), _F32),
            jax.ShapeDtypeStruct((nb, WINDOW, D_KV), _F32),
            jax.ShapeDtypeStruct((CONV_WIDTH - 1, nb, D_RNN), _F32),
            jax.ShapeDtypeStruct((nb, D_RNN), _F32),
        ],
        scratch_shapes=[
            pltpu.VMEM((D_KV // V7X_LANES, N_HEADS * nb, V7X_LANES), _F32),
            pltpu.VMEM((D_KV // V7X_LANES, N_HEADS * nb, V7X_LANES), _F32),
            pltpu.VMEM((nb, D_KV), _F32), pltpu.VMEM((nb, D_KV), _F32),
            pltpu.VMEM((nb, D_RNN), _F32), pltpu.VMEM((nb, D), _F32),
        ],
        compiler_params=pltpu.CompilerParams(
            dimension_semantics=("arbitrary",), vmem_limit_bytes=V7X_VMEM_LIMIT_BYTES),
        name="layer_sample",
    )(x, ck, cv, cst, h0, sinks.reshape(N_HEADS, 1), row(gpre), row(gpost), win_b, wout_b, convw, row(convb),
      wab, row(ba), row(bx), row(lam), row(gpre2), row(gpost2), wup_b, wdn_b)


def _gate_weights(w_a, w_x):
    per = GATE_CHUNK // RNN_BLOCK
    eye = jnp.eye(per, dtype=bool)[None, :, None, :, None]

    def bd(w):
        w = w.reshape(N_GATE_CHUNKS, per, RNN_BLOCK, 1, RNN_BLOCK)
        return jnp.where(eye, w, 0.0).reshape(N_GATE_CHUNKS, GATE_CHUNK, GATE_CHUNK)

    return jnp.concatenate([bd(w_a), bd(w_x)], axis=-1).astype(_BF16)


def kernel(x_prompt, x_sample, cache_k_win, cache_v_win, state_conv, state_lru, w_in, w_out, sinks, conv_w, conv_b,
           lru_w_a, lru_b_a, lru_w_x, lru_b_x, lru_lambda, w_up, w_down, g_pre_mix, g_post_mix, g_pre_ffn,
           g_post_ffn):
    B, T, D = x_prompt.shape
    DB, DT, _ = x_sample.shape
    depth = w_in.shape[0]
    assert depth == 1 and DT == 1 and T % (2 * TQ) == 0 and DB % SAMPLE_BC == 0
    l = 0
    win_b = w_in[l].astype(_BF16)
    wout_b = _col_tiles(w_out[l], V7X_MXU_DIM)
    wup_b = _col_tiles(w_up[l], FFN_UP_COLS)
    wdn_b = _col_tiles(w_down[l], V7X_MXU_DIM)
    wab = _gate_weights(lru_w_a[l], lru_w_x[l])
    layer = (g_pre_mix[l], g_post_mix[l], win_b, wout_b, conv_w[l], conv_b[l], wab, lru_b_a[l], lru_b_x[l],
             lru_lambda[l])

    yp, kp, vp, cp, hp = _layer_prompt(x_prompt, sinks[l], *layer, g_pre_ffn[l], g_post_ffn[l], wup_b, wdn_b)

    ys, kq, vq, cq, hq = _layer_sample(
        x_sample.reshape(DB, D), cache_k_win[l].reshape(DB, WINDOW, D_KV), cache_v_win[l].reshape(DB, WINDOW, D_KV),
        jnp.swapaxes(state_conv[l], 0, 1), state_lru[l], sinks[l], *layer, g_pre_ffn[l], g_post_ffn[l], wup_b, wdn_b)
    ys = ys.reshape(DB, DT, D)

    kv_shape = (1, -1, WINDOW, N_KV_HEADS, HEAD_DIM)
    return (yp, ys, kp.reshape(kv_shape), vp.reshape(kv_shape), cp[None], hp.reshape(1, B, D_RNN),
            kq.reshape(kv_shape), vq.reshape(kv_shape), jnp.swapaxes(cq, 0, 1)[None], hq[None])
```

```python
import functools

import jax
import jax.numpy as jnp
from jax import lax
from jax.experimental import pallas as pl
from jax.experimental.pallas import tpu as pltpu

D_MODEL = 1024
N_HEADS = 16
N_KV_HEADS = 4
HEAD_DIM = 64
GROUP = N_HEADS // N_KV_HEADS
D_ATTN = N_HEADS * HEAD_DIM
D_KV = N_KV_HEADS * HEAD_DIM
WINDOW = 128
D_RNN = D_MODEL
N_RNN_BLOCKS = 16
RNN_BLOCK = D_RNN // N_RNN_BLOCKS
CONV_WIDTH = 4
LRU_C = 8.0
D_FF = 4 * D_MODEL
EPS = 1e-6

OFF_Q = 0
OFF_K = D_ATTN
OFF_V = D_ATTN + D_KV
OFF_U = D_ATTN + 2 * D_KV
OFF_G = OFF_U + D_RNN
OFF_GA = OFF_G + D_RNN
OFF_GR = OFF_GA + D_MODEL
D_IN = OFF_GR + D_MODEL

V7X_SUBLANES = 8
V7X_LANES = 128
V7X_MXU_DIM = 256
V7X_VMEM_BYTES = 64 * 1024 * 1024
V7X_VMEM_LIMIT_BYTES = V7X_VMEM_BYTES - 4 * 1024 * 1024

TQ = WINDOW
GATE_CHUNK = V7X_MXU_DIM
N_GATE_CHUNKS = D_RNN // GATE_CHUNK
SAMPLE_BC = 16
FFN_UP_COLS = 512

_BF16 = jnp.bfloat16
_F32 = jnp.float32


def _dot(a, b):
    return jnp.dot(a, b, preferred_element_type=_F32)


def _dot_nt(a, b):
    return lax.dot_general(a, b, (((1,), (1,)), ((), ())), preferred_element_type=_F32)


def _dot_col_tiles(a, w_ref):
    return jnp.concatenate([_dot(a, w_ref[j]) for j in range(w_ref.shape[0])], axis=1)


def _col_tiles(w, cols):
    k, n = w.shape
    return jnp.transpose(w.astype(_BF16).reshape(k, n // cols, cols), (1, 0, 2))


def _rmsnorm(x, g):
    return x * lax.rsqrt(jnp.mean(x * x, axis=-1, keepdims=True) + EPS) * g


def _softplus(y):
    return jnp.maximum(y, 0.0) + jnp.log1p(jnp.exp(-jnp.abs(y)))


def _lane_block(shape):
    return lax.broadcasted_iota(jnp.int32, shape, len(shape) - 1) // HEAD_DIM


def _lru_gates(uc, wab, ba, bx, lam):
    gc = _dot(uc.astype(_BF16), wab)
    r = jax.nn.sigmoid(gc[:, :GATE_CHUNK] + ba)
    i = jax.nn.sigmoid(gc[:, GATE_CHUNK:] + bx)
    log_a = -LRU_C * r * _softplus(-lam)
    a = jnp.exp(log_a)
    m2 = -jnp.tanh(log_a) * (a * a + 1.0)
    mult = jnp.where(m2 > 0.0, m2 * lax.rsqrt(m2), 0.0)
    return a, mult, i


def _replicate_head_blocks(m):
    lo, hi = m[:, :V7X_LANES], m[:, V7X_LANES:]
    rlo = pltpu.roll(lo, HEAD_DIM, axis=1)
    rhi = pltpu.roll(hi, HEAD_DIM, axis=1)
    first = lax.broadcasted_iota(jnp.int32, lo.shape, 1) < HEAD_DIM
    halves = (jnp.where(first, lo, rlo), jnp.where(first, rlo, lo),
              jnp.where(first, hi, rhi), jnp.where(first, rhi, hi))
    return [jnp.concatenate([b, b], axis=1) for b in halves]


ROW_GPRE, ROW_GPOST, ROW_CONVB, ROW_BA, ROW_BX, ROW_LAM, ROW_GPRE2, ROW_GPOST2 = range(8)


def _layer_prompt_kernel(*refs, nt, n_scratch):
    step = pl.program_id(0)
    @pl.when(step == 0)
    def _():
        for ref in refs[-n_scratch:]:
            ref[...] = jnp.zeros(ref.shape, ref.dtype)

    for par in (0, 1):
        _layer_prompt_step(par, step, *refs, nt=nt)


def _layer_prompt_step(par, step, sinks_ref, xa_ref, xb_ref, rows_ref, win_ref, wout_ref, convw_ref,
                       wab_ref, wup_ref, wdn_ref,
                       y_ref, kwin_ref, vwin_ref, convst_ref, lru_ref,
                       z0_ref, z1_ref, x10_ref, x11_ref, hn0_ref, hn1_ref, kprev_ref, vprev_ref, uext_ref, hcarry_ref,
                       *temps, nt):
    act_ref, f_ref, a_ref, b_ref, h_ref = temps[par::2]
    za_ref, zb_ref = (z0_ref, z1_ref) if par == 0 else (z1_ref, z0_ref)
    x1c_ref, x1b_ref = (x10_ref, x11_ref) if par == 0 else (x11_ref, x10_ref)
    hnc_ref, hnb_ref = (hn0_ref, hn1_ref) if par == 0 else (hn1_ref, hn0_ref)
    rows_a = slice(par * TQ, (par + 1) * TQ)
    x_res = xb_ref[0, TQ:2 * TQ, :] if par == 0 else xa_ref[0, 0:TQ, :]
    rows_y = slice(par * TQ, (par + 1) * TQ)

    prow = lambda i, cols=slice(None): rows_ref[i:i + 1, cols]
    xnb = _rmsnorm(xa_ref[0, rows_a, :], prow(ROW_GPRE)).astype(_BF16)

    def a_tile(j):
        c = j * V7X_MXU_DIM
        za_ref[:, c:c + V7X_MXU_DIM] = _dot(xnb, win_ref[:, c:c + V7X_MXU_DIM])

    def c_up(j):
        c = j * FFN_UP_COLS
        up = _dot(hnc_ref[...], wup_ref[j])
        act_ref[:, c:c + FFN_UP_COLS] = jnp.square(jnp.maximum(up, 0.0)).astype(_BF16)

    def c_down(j):
        c = j * V7X_MXU_DIM
        f_ref[:, c:c + V7X_MXU_DIM] = _dot(act_ref[...], wdn_ref[j])

    def c_tail():
        y_ref[0, rows_y, :] = x1c_ref[...] + _rmsnorm(f_ref[...], prow(ROW_GPOST2))

    n_a, n_up, n_dn = D_IN // V7X_MXU_DIM, D_FF // FFN_UP_COLS, D_MODEL // V7X_MXU_DIM
    w_up, w_dn = FFN_UP_COLS // V7X_MXU_DIM, D_FF // D_MODEL
    a_iter = iter([(1, functools.partial(a_tile, j)) for j in range(n_a)])
    pieces = []
    for j in range(n_up):
        pieces.append((w_up, functools.partial(c_up, j)))
        pieces.extend(p for _, p in zip(range(1), a_iter))
    for j in range(n_dn):
        pieces.append((w_dn, functools.partial(c_down, j)))
        pieces.extend(p for _, p in zip(range(1), a_iter))
    pieces.append((0, c_tail))
    pieces.extend(a_iter)
    pieces.reverse()
    credit = [0.0]

    def fill(units):
        credit[0] += units
        while pieces and credit[0] >= 0.5 * pieces[-1][0]:
            weight, emit = pieces.pop()
            emit()
            credit[0] -= weight

    fill(2)

    t = (2 * step + par + nt - 1) % nt
    assert nt % 2 == 0
    rd, wr = 1 - par, par
    base = V7X_SUBLANES

    k = zb_ref[:, OFF_K:OFF_V]
    v = zb_ref[:, OFF_V:OFF_U]
    if par == 0:
        kwin_ref[0] = k
        vwin_ref[0] = v
    kcur = [a.astype(_BF16) for a in _replicate_head_blocks(k)]
    vt = jnp.transpose(v).astype(_BF16)
    vcur = [vt[h * HEAD_DIM:(h + 1) * HEAD_DIM, :] for h in range(N_KV_HEADS)]
    for h in range(N_KV_HEADS):
        kprev_ref[wr * N_KV_HEADS + h] = kcur[h]
        vprev_ref[wr * N_KV_HEADS + h] = vcur[h]

    kj = lax.broadcasted_iota(jnp.int32, (2 * TQ, TQ), 0)
    qi = lax.broadcasted_iota(jnp.int32, (2 * TQ, TQ), 1)
    diff = qi + TQ - kj
    first_key = jnp.where(t > 0, 0, TQ)
    mask = (diff >= 0) & (diff <= WINDOW) & (kj >= first_key)
    blk = _lane_block((TQ, D_KV))
    scale = HEAD_DIM ** -0.5

    def attention_head(h):
        qc = zb_ref[:, h * D_KV:(h + 1) * D_KV] * scale
        qm = jnp.concatenate([jnp.where(blk == g, qc, 0.0).astype(_BF16) for g in range(GROUP)], axis=0)
        kk = jnp.concatenate([kprev_ref[rd * N_KV_HEADS + h], kcur[h]], axis=0)
        vv = jnp.concatenate([vprev_ref[rd * N_KV_HEADS + h], vcur[h]], axis=1)
        sc = _dot_nt(kk, qm)
        ps, invs = [], []
        for g in range(GROUP):
            sg = jnp.where(mask, sc[:, g * TQ:(g + 1) * TQ], -jnp.inf)
            sink = sinks_ref[h * GROUP + g]
            m = jnp.maximum(jnp.max(sg, axis=0, keepdims=True), sink)
            p = jnp.exp(sg - m)
            den = jnp.sum(p, axis=0, keepdims=True) + jnp.exp(sink - m)
            invs.append(1.0 / den)
            ps.append(p.astype(_BF16))
            if g % 2 == 1:
                fill(1)
        o = _dot(vv, jnp.concatenate(ps, axis=1))
        o = jnp.concatenate([o[:, g * TQ:(g + 1) * TQ] * invs[g] for g in range(GROUP)], axis=0)
        fill(1)
        return jnp.transpose(o)

    def scan_groups(carry, first, last):
        sub = lax.broadcasted_iota(jnp.int32, (V7X_SUBLANES, D_RNN), 0)
        for j in range(first, last):
            rows = slice(j * V7X_SUBLANES, (j + 1) * V7X_SUBLANES)
            aj = a_ref[rows, :]
            bj = b_ref[rows, :]
            for sft in (1, 2, 4):
                keep = sub >= sft
                bj = jnp.where(keep, aj * pltpu.roll(bj, sft, axis=0) + bj, bj)
                aj = jnp.where(keep, aj * pltpu.roll(aj, sft, axis=0), aj)
            hj = aj * carry + bj
            h_ref[rows, :] = hj
            if j % 2 == 1:
                fill(2 if j % 4 == 1 else 1)
            carry = jnp.broadcast_to(hj[V7X_SUBLANES - 1:V7X_SUBLANES, :], (V7X_SUBLANES, D_RNN))
        return carry

    pos0 = (t * TQ + lax.broadcasted_iota(jnp.int32, (TQ, GATE_CHUNK), 0)) == 0
    for c in range(N_GATE_CHUNKS):
        cols = slice(c * GATE_CHUNK, (c + 1) * GATE_CHUNK)
        u = zb_ref[:, OFF_U + c * GATE_CHUNK:OFF_U + (c + 1) * GATE_CHUNK]
        uext_ref[rd, 0:base, cols] = jnp.where(t > 0, uext_ref[rd, 0:base, cols], 0.0)
        uext_ref[rd, base:base + TQ, cols] = u
        cw = convw_ref[:, cols]
        ue = uext_ref[rd, :, cols]
        cs = pltpu.roll(ue, 3, axis=0)[base:, :] * cw[0:1]
        cs = cs + pltpu.roll(ue, 2, axis=0)[base:, :] * cw[1:2]
        cs = cs + pltpu.roll(ue, 1, axis=0)[base:, :] * cw[2:3]
        cs = cs + u * cw[3:4]
        uc = prow(ROW_CONVB, cols) + cs
        if par == 0:
            convst_ref[0, :, cols] = uext_ref[rd, base + TQ - (CONV_WIDTH - 1):base + TQ, cols]
        uext_ref[wr, 0:base, cols] = u[TQ - base:TQ, :]
        fill(2)
        a, mult, i = _lru_gates(uc, wab_ref[c], prow(ROW_BA, cols), prow(ROW_BX, cols), prow(ROW_LAM, cols))
        a_ref[:, cols] = a
        b_ref[:, cols] = jnp.where(pos0, 1.0, mult) * i * uc
        fill(2)

    attn_chunks = [attention_head(0), attention_head(1)]
    n_groups = TQ // V7X_SUBLANES
    carry = scan_groups(jnp.where(t > 0, hcarry_ref[rd], 0.0), 0, n_groups // 2)
    attn_chunks.append(attention_head(2))
    carry = scan_groups(carry, n_groups // 2, n_groups)
    hcarry_ref[wr] = carry
    if par == 0:
        lru_ref[0] = carry[0:1, :]
    attn_chunks.append(attention_head(3))

    merged = []
    for c in range(D_MODEL // V7X_MXU_DIM):
        cols = slice(c * V7X_MXU_DIM, (c + 1) * V7X_MXU_DIM)
        zcol = lambda off: zb_ref[:, off + c * V7X_MXU_DIM:off + (c + 1) * V7X_MXU_DIM]
        rnn_out = jax.nn.gelu(zcol(OFF_G)) * h_ref[:, cols]
        merged.append((jax.nn.sigmoid(zcol(OFF_GA)) * attn_chunks[c]
                       + jax.nn.sigmoid(zcol(OFF_GR)) * rnn_out).astype(_BF16))
        fill(1)
    mix = _dot_col_tiles(jnp.concatenate(merged, axis=1), wout_ref)
    fill(3)
    x1 = x_res + _rmsnorm(mix, prow(ROW_GPOST))
    x1b_ref[...] = x1
    hnb_ref[...] = _rmsnorm(x1, prow(ROW_GPRE2)).astype(_BF16)
    fill(sum(w for w, _ in pieces) + len(pieces))
    assert not pieces


def _const_spec(shape):
    zeros = (0,) * len(shape)
    return pl.BlockSpec(shape, lambda *_: zeros, pipeline_mode=pl.Buffered(1))


def _layer_prompt(x, sinks, gpre, gpost, win_b, wout_b, convw, convb, wab, ba, bx, lam, gpre2, gpost2, wup_b, wdn_b):
    B, T, D = x.shape
    nt = T // TQ
    n_blocks = B * nt
    rows = jnp.stack([gpre, gpost, convb, ba, bx, lam, gpre2, gpost2])
    n_pairs = n_blocks // 2
    ntp = nt // 2
    cur = lambda s: jnp.minimum(s, n_pairs - 1)
    prv = lambda s: jnp.maximum(s - 1, 0)
    row_b = lambda s: jnp.maximum(2 * s - 1, 0) // nt
    scratch_shapes = [
        pltpu.VMEM((TQ, D_IN), _F32), pltpu.VMEM((TQ, D_IN), _F32),
        pltpu.VMEM((TQ, D), _F32), pltpu.VMEM((TQ, D), _F32),
        pltpu.VMEM((TQ, D), _BF16), pltpu.VMEM((TQ, D), _BF16),
        pltpu.VMEM((2 * N_KV_HEADS, TQ, D_KV), _BF16),
        pltpu.VMEM((2 * N_KV_HEADS, HEAD_DIM, TQ), _BF16),
        pltpu.VMEM((2, V7X_SUBLANES + TQ, D_RNN), _F32),
        pltpu.VMEM((2, V7X_SUBLANES, D_RNN), _F32),
        *[pltpu.VMEM(shape, dtype) for shape, dtype in
          (((TQ, D_FF), _BF16), ((TQ, D), _F32), ((TQ, D_RNN), _F32), ((TQ, D_RNN), _F32), ((TQ, D_RNN), _F32))
          for _ in range(2)],
    ]
    return pl.pallas_call(
        functools.partial(_layer_prompt_kernel, nt=nt, n_scratch=len(scratch_shapes)),
        grid=(n_pairs + 1,),
        in_specs=[
            pl.BlockSpec(memory_space=pltpu.SMEM),
            pl.BlockSpec((1, 2 * TQ, D), lambda s: (cur(s) // ntp, cur(s) % ntp, 0)),
            pl.BlockSpec((1, 2 * TQ, D), lambda s: (prv(s) // ntp, prv(s) % ntp, 0)),
            _const_spec((8, D)),
            _const_spec((D, D_IN)), _const_spec((D // V7X_MXU_DIM, D, V7X_MXU_DIM)),
            _const_spec((CONV_WIDTH, D_RNN)),
            _const_spec((N_GATE_CHUNKS, GATE_CHUNK, 2 * GATE_CHUNK)),
            _const_spec((D_FF // FFN_UP_COLS, D, FFN_UP_COLS)), _const_spec((D // V7X_MXU_DIM, D_FF, V7X_MXU_DIM)),
        ],
        out_specs=[
            pl.BlockSpec((1, 2 * TQ, D), lambda s: (prv(s) // ntp, prv(s) % ntp, 0)),
            pl.BlockSpec((1, WINDOW, D_KV), lambda s: (row_b(s), 0, 0)),
            pl.BlockSpec((1, WINDOW, D_KV), lambda s: (row_b(s), 0, 0)),
            pl.BlockSpec((1, CONV_WIDTH - 1, D_RNN), lambda s: (row_b(s), 0, 0)),
            pl.BlockSpec((1, 1, D_RNN), lambda s: (row_b(s), 0, 0)),
        ],
        out_shape=[
            jax.ShapeDtypeStruct((B, T, D), _F32),
            jax.ShapeDtypeStruct((B, WINDOW, D_KV), _F32),
            jax.ShapeDtypeStruct((B, WINDOW, D_KV), _F32),
            jax.ShapeDtypeStruct((B, CONV_WIDTH - 1, D_RNN), _F32),
            jax.ShapeDtypeStruct((B, 1, D_RNN), _F32),
        ],
        scratch_shapes=scratch_shapes,
        compiler_params=pltpu.CompilerParams(
            dimension_semantics=("arbitrary",), vmem_limit_bytes=V7X_VMEM_LIMIT_BYTES),
        name="layer_prompt",
    )(sinks, x, x, rows, win_b, wout_b, convw, wab, wup_b, wdn_b)


def _layer_sample_kernel(x_ref, ck_ref, cv_ref, cst_ref, h0_ref, sinks_ref, gpre_ref, gpost_ref, win_ref, wout_ref,
                         convw_ref, convb_ref, wab_ref, ba_ref, bx_ref, lam_ref, gpre2_ref, gpost2_ref, wup_ref, wdn_ref,
                         y_ref, kwin_ref, vwin_ref, convst_ref, lru_ref,
                       qrow_ref, orow_ref, knew_ref, vnew_ref, rnn_ref, siga_ref):
    step = pl.program_id(0)
    last = pl.num_programs(0) - 1
    nb = x_ref.shape[0]
    blk = _lane_block((nb, D_KV))

    @pl.when(step == 0)
    def _():
        xnb = _rmsnorm(x_ref[...], gpre_ref[...]).astype(_BF16)
        qkv = _dot(xnb, win_ref[:, OFF_Q:OFF_U])
        knew_ref[...] = qkv[:, OFF_K:OFF_V]
        vnew_ref[...] = qkv[:, OFF_V:OFF_U]
        scale = HEAD_DIM ** -0.5
        for h in range(N_KV_HEADS):
            qc = qkv[:, h * D_KV:(h + 1) * D_KV] * scale
            for g in range(GROUP):
                shift = ((h - g) % GROUP) * HEAD_DIM
                moved = pltpu.roll(qc, shift, axis=1) if shift else qc
                hd = h * GROUP + g
                placed = jnp.where(blk == h, moved, 0.0)
                for half in range(D_KV // V7X_LANES):
                    qrow_ref[half, hd * nb:(hd + 1) * nb, :] = placed[:, half * V7X_LANES:(half + 1) * V7X_LANES]

        u = _dot(xnb, win_ref[:, OFF_U:OFF_G])
        cw = convw_ref[...]
        cs = cst_ref[0] * cw[0:1]
        cs = cs + cst_ref[1] * cw[1:2]
        cs = cs + cst_ref[2] * cw[2:3]
        cs = cs + u * cw[3:4]
        uc = convb_ref[...] + cs
        convst_ref[0] = cst_ref[1]
        convst_ref[1] = cst_ref[2]
        convst_ref[2] = u
        parts = [_lru_gates(uc[:, c * GATE_CHUNK:(c + 1) * GATE_CHUNK], wab_ref[c],
                            *(r[:, c * GATE_CHUNK:(c + 1) * GATE_CHUNK] for r in (ba_ref, bx_ref, lam_ref)))
                 for c in range(N_GATE_CHUNKS)]
        a, mult, i = (jnp.concatenate(p, axis=1) for p in zip(*parts))
        hnew = a * h0_ref[...] + mult * i * uc
        lru_ref[...] = hnew
        g_branch = _dot(xnb, win_ref[:, OFF_G:OFF_GA])
        gate_r = _dot(xnb, win_ref[:, OFF_GR:D_IN])
        rnn_ref[...] = jax.nn.sigmoid(gate_r) * (jax.nn.gelu(g_branch) * hnew)
        siga_ref[...] = jax.nn.sigmoid(_dot(xnb, win_ref[:, OFF_GA:OFF_GR]))

    sink = sinks_ref[...]
    rowid = lax.broadcasted_iota(jnp.int32, (WINDOW, D_KV), 0)

    def per_row(bl, carry):
        b = step * SAMPLE_BC + bl
        kb = ck_ref[bl]
        vb = cv_ref[bl]
        kn = knew_ref[pl.ds(b, 1), :]
        vn = vnew_ref[pl.ds(b, 1), :]
        kwin_ref[bl] = jnp.where(rowid == WINDOW - 1, kn, pltpu.roll(kb, WINDOW - 1, axis=0))
        vwin_ref[bl] = jnp.where(rowid == WINDOW - 1, vn, pltpu.roll(vb, WINDOW - 1, axis=0))

        head_rows = pl.ds(b, N_HEADS, stride=nb)
        qm = jnp.concatenate([qrow_ref[half, head_rows, :] for half in range(D_KV // V7X_LANES)],
                             axis=1).astype(_BF16)
        s = _dot_nt(qm, kb.astype(_BF16))
        knr = kn.astype(_BF16).astype(_F32)
        s_self = jnp.sum(qm.astype(_F32) * knr, axis=-1, keepdims=True)
        m = jnp.maximum(jnp.maximum(jnp.max(s, axis=-1, keepdims=True), s_self), sink)
        p = jnp.exp(s - m)
        p_self = jnp.exp(s_self - m)
        den = jnp.sum(p, axis=-1, keepdims=True) + p_self + jnp.exp(sink - m)
        o = _dot(p.astype(_BF16), vb.astype(_BF16))
        o = o + p_self.astype(_BF16).astype(_F32) * vn.astype(_BF16).astype(_F32)
        o = o * (1.0 / den)
        for half in range(D_KV // V7X_LANES):
            orow_ref[half, head_rows, :] = o[:, half * V7X_LANES:(half + 1) * V7X_LANES]
        return carry

    lax.fori_loop(0, SAMPLE_BC, per_row, 0, unroll=True)

    @pl.when(step == last)
    def _():
        chunks = []
        for h in range(N_KV_HEADS):
            acc = None
            for g in range(GROUP):
                hd = h * GROUP + g
                rows = jnp.concatenate([orow_ref[half, hd * nb:(hd + 1) * nb, :]
                                        for half in range(D_KV // V7X_LANES)], axis=1)
                sel = jnp.where(blk == h, rows, 0.0)
                shift = ((g - h) % GROUP) * HEAD_DIM
                moved = pltpu.roll(sel, shift, axis=1) if shift else sel
                acc = moved if acc is None else acc + moved
            chunks.append(acc)
        attn = jnp.concatenate(chunks, axis=1)
        merged = (siga_ref[...] * attn + rnn_ref[...]).astype(_BF16)
        mix = _dot_col_tiles(merged, wout_ref)
        x1 = x_ref[...] + _rmsnorm(mix, gpost_ref[...])
        hn = _rmsnorm(x1, gpre2_ref[...]).astype(_BF16)
        act = jnp.square(jnp.maximum(_dot_col_tiles(hn, wup_ref), 0.0)).astype(_BF16)
        y_ref[...] = x1 + _rmsnorm(_dot_col_tiles(act, wdn_ref), gpost2_ref[...])


def _layer_sample(x, ck, cv, cst, h0, sinks, gpre, gpost, win_b, wout_b, convw, convb, wab, ba, bx, lam,
                  gpre2, gpost2, wup_b, wdn_b):
    nb, D = x.shape
    row = lambda a: a.reshape(1, -1)
    cache_spec = pl.BlockSpec((SAMPLE_BC, WINDOW, D_KV), lambda s: (s, 0, 0))
    return pl.pallas_call(
        _layer_sample_kernel,
        grid=(nb // SAMPLE_BC,),
        in_specs=[
            _const_spec((nb, D)), cache_spec, cache_spec,
            _const_spec((CONV_WIDTH - 1, nb, D_RNN)), _const_spec((nb, D_RNN)),
            _const_spec((N_HEADS, 1)),
            _const_spec((1, D)), _const_spec((1, D)),
            _const_spec((D, D_IN)), _const_spec((D // V7X_MXU_DIM, D, V7X_MXU_DIM)),
            _const_spec((CONV_WIDTH, D_RNN)), _const_spec((1, D_RNN)),
            _const_spec((N_GATE_CHUNKS, GATE_CHUNK, 2 * GATE_CHUNK)),
            _const_spec((1, D_RNN)), _const_spec((1, D_RNN)), _const_spec((1, D_RNN)),
            _const_spec((1, D)), _const_spec((1, D)),
            _const_spec((D_FF // FFN_UP_COLS, D, FFN_UP_COLS)), _const_spec((D // V7X_MXU_DIM, D_FF, V7X_MXU_DIM)),
        ],
        out_specs=[
            pl.BlockSpec((nb, D), lambda s: (0, 0)),
            cache_spec, cache_spec,
            pl.BlockSpec((CONV_WIDTH - 1, nb, D_RNN), lambda s: (0, 0, 0)),
            pl.BlockSpec((nb, D_RNN), lambda s: (0, 0)),
        ],
        out_shape=[
            jax.ShapeDtypeStruct((nb, D), _F32),
            jax.ShapeDtypeStruct((nb, WINDOW, D_KV), _F32),
            jax.ShapeDtypeStruct((nb, WINDOW, D_KV), _F32),
            jax.ShapeDtypeStruct((CONV_WIDTH - 1, nb, D_RNN), _F32),
            jax.ShapeDtypeStruct((nb, D_RNN), _F32),
        ],
        scratch_shapes=[
            pltpu.VMEM((D_KV // V7X_LANES, N_HEADS * nb, V7X_LANES), _F32),
            pltpu.VMEM((D_KV // V7X_LANES, N_HEADS * nb, V7X_LANES), _F32),
            pltpu.VMEM((nb, D_KV), _F32), pltpu.VMEM((nb, D_KV), _F32),
            pltpu.VMEM((nb, D_RNN), _F32), pltpu.VMEM((nb, D), _F32),
        ],
        compiler_params=pltpu.CompilerParams(
            dimension_semantics=("arbitrary",), vmem_limit_bytes=V7X_VMEM_LIMIT_BYTES),
        name="layer_sample",
    )(x, ck, cv, cst, h0, sinks.reshape(N_HEADS, 1), row(gpre), row(gpost), win_b, wout_b, convw, row(convb),
      wab, row(ba), row(bx), row(lam), row(gpre2), row(gpost2), wup_b, wdn_b)


def _gate_weights(w_a, w_x):
    per = GATE_CHUNK // RNN_BLOCK
    eye = jnp.eye(per, dtype=bool)[None, :, None, :, None]

    def bd(w):
        w = w.reshape(N_GATE_CHUNKS, per, RNN_BLOCK, 1, RNN_BLOCK)
        return jnp.where(eye, w, 0.0).reshape(N_GATE_CHUNKS, GATE_CHUNK, GATE_CHUNK)

    return jnp.concatenate([bd(w_a), bd(w_x)], axis=-1).astype(_BF16)


def kernel(x_prompt, x_sample, cache_k_win, cache_v_win, state_conv, state_lru, w_in, w_out, sinks, conv_w, conv_b,
           lru_w_a, lru_b_a, lru_w_x, lru_b_x, lru_lambda, w_up, w_down, g_pre_mix, g_post_mix, g_pre_ffn,
           g_post_ffn):
    B, T, D = x_prompt.shape
    DB, DT, _ = x_sample.shape
    depth = w_in.shape[0]
    assert depth == 1 and DT == 1 and T % (2 * TQ) == 0 and DB % SAMPLE_BC == 0
    l = 0
    win_b = w_in[l].astype(_BF16)
    wout_b = _col_tiles(w_out[l], V7X_MXU_DIM)
    wup_b = _col_tiles(w_up[l], FFN_UP_COLS)
    wdn_b = _col_tiles(w_down[l], V7X_MXU_DIM)
    wab = _gate_weights(lru_w_a[l], lru_w_x[l])
    layer = (g_pre_mix[l], g_post_mix[l], win_b, wout_b, conv_w[l], conv_b[l], wab, lru_b_a[l], lru_b_x[l],
             lru_lambda[l])

    yp, kp, vp, cp, hp = _layer_prompt(x_prompt, sinks[l], *layer, g_pre_ffn[l], g_post_ffn[l], wup_b, wdn_b)

    ys, kq, vq, cq, hq = _layer_sample(
        x_sample.reshape(DB, D), cache_k_win[l].reshape(DB, WINDOW, D_KV), cache_v_win[l].reshape(DB, WINDOW, D_KV),
        jnp.swapaxes(state_conv[l], 0, 1), state_lru[l], sinks[l], *layer, g_pre_ffn[l], g_post_ffn[l], wup_b, wdn_b)
    ys = ys.reshape(DB, DT, D)

    kv_shape = (1, -1, WINDOW, N_KV_HEADS, HEAD_DIM)
    return (yp, ys, kp.reshape(kv_shape), vp.reshape(kv_shape), cp[None], hp.reshape(1, B, D_RNN),
            kq.reshape(kv_shape), vq.reshape(kv_shape), jnp.swapaxes(cq, 0, 1)[None], hq[None])
```
